```python
import math
import jax, jax.numpy as jnp
from jax import lax
import numpy as np

D_MODEL = 1024
BATCH = 16
SEQ = 256
DEPTH = 1
DEC_BATCH = 4
DEC_SEQ = 4096
PAST_LEN = 512

GRID_W = 64
HEAD_DIM = 64
N_HEADS = 8
KV_HEADS = 2
REP = N_HEADS // KV_HEADS
ATTN_WIDTH = N_HEADS * HEAD_DIM
KV_WIDTH = KV_HEADS * HEAD_DIM
WINDOW = 128
BLOCK = 128
ROPE_THETA = 10000.0
ROPE_FREQS = HEAD_DIM // 4
SSM_CH = 16
SSM_WIDTH = D_MODEL - ATTN_WIDTH
SSM_GROUPS = SSM_WIDTH // SSM_CH
SSM_STATE = 64
N_DIRS = 2
IN_WIDTH = ATTN_WIDTH + 2 * KV_WIDTH + SSM_WIDTH
D_FF = 2816
CONV_W = 3
EPS = 1e-6
NEG_INF = -1e30

kernel_name = "hymba_s5_prefix_dit_step"


def rms_norm(x, g):
    xf = x.astype(jnp.float32)
    y = xf * lax.rsqrt(jnp.mean(xf * xf, axis=-1, keepdims=True) + EPS)
    return (y * g.astype(jnp.float32)).astype(x.dtype)


def adaln(cond, w_ada, b_ada):
    mod = jax.nn.silu(cond) @ w_ada + b_ada
    return jnp.split(mod, 6, axis=-1)


def modulate(x, g, shift, scale):
    return rms_norm(x, g) * (1 + scale[:, None, :]) + shift[:, None, :]


def in_projection(h, w_in):
    b, n = h.shape[:2]
    proj = h @ w_in
    q, k, v, u = jnp.split(proj, [ATTN_WIDTH, ATTN_WIDTH + KV_WIDTH, ATTN_WIDTH + 2 * KV_WIDTH], axis=-1)
    return (q.reshape(b, n, N_HEADS, HEAD_DIM), k.reshape(b, n, KV_HEADS, HEAD_DIM),
            v.reshape(b, n, KV_HEADS, HEAD_DIM), u)


def axial_rope_angles(n_tokens):
    rows = n_tokens // GRID_W
    row = jnp.repeat(jnp.arange(rows, dtype=jnp.float32), GRID_W)
    col = jnp.tile(jnp.arange(GRID_W, dtype=jnp.float32), rows)
    inv_freq = ROPE_THETA ** (-jnp.arange(ROPE_FREQS, dtype=jnp.float32) / ROPE_FREQS)
    return row[:, None] * inv_freq[None, :], col[:, None] * inv_freq[None, :]


def _rotate_half(v, ang):
    v1, v2 = jnp.split(v, 2, axis=-1)
    cos = jnp.cos(ang)[None, :, None, :]
    sin = jnp.sin(ang)[None, :, None, :]
    return jnp.concatenate([v1 * cos - v2 * sin, v1 * sin + v2 * cos], axis=-1)


def apply_axial_rope(x, ang_row, ang_col):
    xf = x.astype(jnp.float32)
    half = HEAD_DIM // 2
    out = jnp.concatenate([_rotate_half(xf[..., :half], ang_row),
                           _rotate_half(xf[..., half:], ang_col)], axis=-1)
    return out.astype(x.dtype)


def _sink_column(sink, b):
    s = sink.astype(jnp.float32).reshape(1, KV_HEADS, REP, 1, 1)
    return jnp.broadcast_to(s, (b, KV_HEADS, REP, BLOCK, 1))


def context_attention(q, k, v, sink):
    b, lc = q.shape[:2]
    nq = lc // BLOCK
    scale = HEAD_DIM ** -0.5
    qb = q.reshape(b, nq, BLOCK, KV_HEADS, REP, HEAD_DIM).transpose(1, 0, 2, 3, 4, 5)
    sink_col = _sink_column(sink, b)

    def one_block(qi):
        s = jnp.einsum("bqkrd,bskd->bkrqs", qi, k).astype(jnp.float32) * scale
        p = jax.nn.softmax(jnp.concatenate([s, sink_col], axis=-1), axis=-1)[..., :-1]
        return jnp.einsum("bkrqs,bskd->bqkrd", p.astype(v.dtype), v)

    o = lax.map(one_block, qb)
    return o.transpose(1, 0, 2, 3, 4, 5).reshape(b, lc, ATTN_WIDTH)


def latent_attention(q, k, v, ck, cv, sink):
    b, n = q.shape[:2]
    nb = n // BLOCK
    scale = HEAD_DIM ** -0.5
    qb = q.reshape(b, nb, BLOCK, KV_HEADS, REP, HEAD_DIM).transpose(1, 0, 2, 3, 4, 5)

    def windows(t):
        tp = jnp.pad(t, ((0, 0), (BLOCK, BLOCK), (0, 0), (0, 0))).reshape(b, nb + 2, BLOCK, KV_HEADS, HEAD_DIM)
        w = jnp.concatenate([tp[:, :-2], tp[:, 1:-1], tp[:, 2:]], axis=2)
        return w.transpose(1, 0, 2, 3, 4)

    kw, vw = windows(k), windows(v)
    r_idx = jnp.arange(BLOCK)[:, None]
    w_idx = jnp.arange(3 * BLOCK)[None, :]
    band = (w_idx >= r_idx + BLOCK - WINDOW) & (w_idx <= r_idx + BLOCK + WINDOW)
    sink_col = _sink_column(sink, b)

    def one_block(args):
        qi, ki, vi, blk = args
        j = (blk - 1) * BLOCK + w_idx
        valid = band & (j >= 0) & (j < n)
        s_loc = jnp.einsum("bqkrd,bskd->bkrqs", qi, ki).astype(jnp.float32) * scale
        s_loc = jnp.where(valid[None, None, None], s_loc, NEG_INF)
        s_ctx = jnp.einsum("bqkrd,bskd->bkrqs", qi, ck).astype(jnp.float32) * scale
        p = jax.nn.softmax(jnp.concatenate([s_loc, s_ctx, sink_col], axis=-1), axis=-1)
        p_loc = p[..., :3 * BLOCK].astype(vi.dtype)
        p_ctx = p[..., 3 * BLOCK:-1].astype(cv.dtype)
        return (jnp.einsum("bkrqs,bskd->bqkrd", p_loc, vi)
                + jnp.einsum("bkrqs,bskd->bqkrd", p_ctx, cv))

    o = lax.map(one_block, (qb, kw, vw, jnp.arange(nb)))
    return o.transpose(1, 0, 2, 3, 4, 5).reshape(b, n, ATTN_WIDTH)


def zoh_discretize(lam_re, lam_im, log_dt, b_re, b_im):
    lam_re = lam_re.astype(jnp.float32)
    lam_im = lam_im.astype(jnp.float32)
    dt = jnp.exp(log_dt.astype(jnp.float32))[:, None]
    mag = jnp.exp(lam_re * dt)
    ang = lam_im * dt
    a_re, a_im = mag * jnp.cos(ang), mag * jnp.sin(ang)
    den = lam_re * lam_re + lam_im * lam_im
    num_re = a_re - 1.0
    coef_re = (num_re * lam_re + a_im * lam_im) / den
    coef_im = (a_im * lam_re - num_re * lam_im) / den
    br, bi = b_re.astype(jnp.float32), b_im.astype(jnp.float32)
    bb_re = coef_re[..., None] * br - coef_im[..., None] * bi
    bb_im = coef_re[..., None] * bi + coef_im[..., None] * br
    return a_re, a_im, bb_re, bb_im


def _combine(e1, e2):
    a1r, a1i, b1r, b1i = e1
    a2r, a2i, b2r, b2i = e2
    return (a2r * a1r - a2i * a1i, a2r * a1i + a2i * a1r,
            a2r * b1r - a2i * b1i + b2r, a2r * b1i + a2i * b1r + b2i)


def diagonal_scan(u, a_re, a_im, bb_re, bb_im, h0, reverse):
    bu_re = jnp.einsum("bngp,gsp->bngs", u, bb_re)
    bu_im = jnp.einsum("bngp,gsp->bngs", u, bb_im)
    if h0 is not None:
        h0_re, h0_im = h0
        edge = -1 if reverse else 0
        bu_re = bu_re.at[:, edge].add(a_re * h0_re - a_im * h0_im)
        bu_im = bu_im.at[:, edge].add(a_re * h0_im + a_im * h0_re)
    ar = jnp.broadcast_to(a_re, bu_re.shape)
    ai = jnp.broadcast_to(a_im, bu_im.shape)
    _, _, h_re, h_im = lax.associative_scan(_combine, (ar, ai, bu_re, bu_im), reverse=reverse, axis=1)
    return h_re, h_im


def s5_bidirectional(u, params, h0_re, h0_im, return_final):
    lam_re, lam_im, log_dt, b_re, b_im, c_re, c_im, d, w_glu = params
    b, n = u.shape[:2]
    uf = u.astype(jnp.float32).reshape(b, n, SSM_GROUPS, SSM_CH)
    y = uf * d.astype(jnp.float32)
    fin_re, fin_im = [], []
    for dr, reverse in enumerate((False, True)):
        a_re, a_im, bb_re, bb_im = zoh_discretize(lam_re[dr], lam_im[dr], log_dt[dr], b_re[dr], b_im[dr])
        h0 = None if h0_re is None else (h0_re[:, dr].astype(jnp.float32), h0_im[:, dr].astype(jnp.float32))
        h_re, h_im = diagonal_scan(uf, a_re, a_im, bb_re, bb_im, h0, reverse)
        y = (y + jnp.einsum("bngs,gps->bngp", h_re, c_re[dr].astype(jnp.float32))
             - jnp.einsum("bngs,gps->bngp", h_im, c_im[dr].astype(jnp.float32)))
        if return_final:
            edge = 0 if reverse else -1
            fin_re.append(h_re[:, edge])
            fin_im.append(h_im[:, edge])
    x = jax.nn.gelu(y)
    out = x * jax.nn.sigmoid(jnp.einsum("bngp,gpq->bngq", x, w_glu.astype(jnp.float32)))
    out = out.reshape(b, n, SSM_WIDTH).astype(u.dtype)
    if return_final:
        return out, jnp.stack(fin_re, axis=1).astype(u.dtype), jnp.stack(fin_im, axis=1).astype(u.dtype)
    return out


def merge_groups(attn, ssm, g_attn, g_ssm, w_out):
    return jnp.concatenate([rms_norm(attn, g_attn), rms_norm(ssm, g_ssm)], axis=-1) @ w_out


def conv_ffn(h, w_up, conv_w, conv_b, w_down):
    n = h.shape[1]
    pad = CONV_W // 2
    up = jnp.pad(h @ w_up, ((0, 0), (pad, pad), (0, 0)))
    acc = up[:, 0:n] * conv_w[0]
    for i in range(1, CONV_W):
        acc = acc + up[:, i:i + n] * conv_w[i]
    g, val = jnp.split(acc + conv_b, 2, axis=-1)
    return (jax.nn.silu(g) * val) @ w_down


def setup_inputs(seed: int = 0) -> dict:
    key = jax.random.key(seed)
    ks = jax.random.split(key, 40)
    f32 = jnp.float32

    def nrm(k, shape, s):
        return jax.random.normal(k, shape, f32) * s

    n_idx = jnp.arange(SSM_STATE, dtype=f32)
    ssm_shape = (DEPTH, N_DIRS, SSM_GROUPS, SSM_STATE)
    return {
        "x_prompt": nrm(ks[0], (BATCH, SEQ, D_MODEL), 1.0),
        "x_sample": nrm(ks[1], (DEC_BATCH, DEC_SEQ, D_MODEL), 1.0),
        "cache_k": nrm(ks[2], (DEC_BATCH, DEPTH, PAST_LEN, KV_HEADS, HEAD_DIM), 1.0),
        "cache_v": nrm(ks[3], (DEC_BATCH, DEPTH, PAST_LEN, KV_HEADS, HEAD_DIM), 1.0),
        "state_ssm_re": nrm(ks[4], (DEC_BATCH, DEPTH, N_DIRS, SSM_GROUPS, SSM_STATE), 0.5),
        "state_ssm_im": nrm(ks[5], (DEC_BATCH, DEPTH, N_DIRS, SSM_GROUPS, SSM_STATE), 0.5),
        "c": nrm(ks[6], (DEC_BATCH, D_MODEL), 1.0),
        "c_ctx": nrm(ks[7], (D_MODEL,), 1.0),
        "w_ada": nrm(ks[8], (DEPTH, D_MODEL, 6 * D_MODEL), 0.5 * D_MODEL ** -0.5),
        "b_ada": nrm(ks[9], (DEPTH, 6 * D_MODEL), 0.01),
        "g_norm1": 1.0 + nrm(ks[10], (DEPTH, D_MODEL), 0.02),
        "g_norm2": 1.0 + nrm(ks[11], (DEPTH, D_MODEL), 0.02),
        "w_in": nrm(ks[12], (DEPTH, D_MODEL, IN_WIDTH), D_MODEL ** -0.5),
        "attn_sink": nrm(ks[13], (DEPTH, N_HEADS), 0.5),
        "ssm_lam_re": -0.5 + nrm(ks[14], ssm_shape, 0.01),
        "ssm_lam_im": math.pi * n_idx + nrm(ks[15], ssm_shape, 0.01),
        "ssm_log_dt": jax.random.uniform(ks[16], (DEPTH, N_DIRS, SSM_GROUPS), f32, math.log(1e-3), math.log(1e-1)),
        "ssm_b_re": nrm(ks[17], (DEPTH, N_DIRS, SSM_GROUPS, SSM_STATE, SSM_CH), (2 * SSM_CH) ** -0.5),
        "ssm_b_im": nrm(ks[18], (DEPTH, N_DIRS, SSM_GROUPS, SSM_STATE, SSM_CH), (2 * SSM_CH) ** -0.5),
        "ssm_c_re": nrm(ks[19], (DEPTH, N_DIRS, SSM_GROUPS, SSM_CH, SSM_STATE), (2 * SSM_STATE) ** -0.5),
        "ssm_c_im": nrm(ks[20], (DEPTH, N_DIRS, SSM_GROUPS, SSM_CH, SSM_STATE), (2 * SSM_STATE) ** -0.5),
        "ssm_d": nrm(ks[21], (DEPTH, SSM_GROUPS, SSM_CH), 1.0),
        "ssm_w_glu": nrm(ks[22], (DEPTH, SSM_GROUPS, SSM_CH, SSM_CH), SSM_CH ** -0.5),
        "g_out_attn": 1.0 + nrm(ks[23], (DEPTH, ATTN_WIDTH), 0.02),
        "g_out_ssm": 1.0 + nrm(ks[24], (DEPTH, SSM_WIDTH), 0.02),
        "w_out": nrm(ks[25], (DEPTH, D_MODEL, D_MODEL), D_MODEL ** -0.5),
        "w_up": nrm(ks[26], (DEPTH, D_MODEL, 2 * D_FF), D_MODEL ** -0.5),
        "conv_w": nrm(ks[27], (DEPTH, CONV_W, 2 * D_FF), CONV_W ** -0.5),
        "conv_b": nrm(ks[28], (DEPTH, 2 * D_FF), 0.01),
        "w_down": nrm(ks[29], (DEPTH, D_FF, D_MODEL), D_FF ** -0.5),
        "g_final": 1.0 + nrm(ks[30], (D_MODEL,), 0.02),
    }


def reference(x_prompt, x_sample, cache_k, cache_v, state_ssm_re, state_ssm_im, c, c_ctx,
              w_ada, b_ada, g_norm1, g_norm2, w_in, attn_sink,
              ssm_lam_re, ssm_lam_im, ssm_log_dt, ssm_b_re, ssm_b_im, ssm_c_re, ssm_c_im,
              ssm_d, ssm_w_glu, g_out_attn, g_out_ssm, w_out, w_up, conv_w, conv_b, w_down, g_final):
    cond_ctx = jnp.broadcast_to(c_ctx[None, :], (x_prompt.shape[0], D_MODEL))
    ang_row, ang_col = axial_rope_angles(x_sample.shape[1])
    xp, xs = x_prompt, x_sample
    new_k, new_v, new_s_re, new_s_im = [], [], [], []
    for l in range(DEPTH):
        ssm_l = (ssm_lam_re[l], ssm_lam_im[l], ssm_log_dt[l], ssm_b_re[l], ssm_b_im[l],
                 ssm_c_re[l], ssm_c_im[l], ssm_d[l], ssm_w_glu[l])

        sh1, sc1, gt1, sh2, sc2, gt2 = adaln(cond_ctx, w_ada[l], b_ada[l])
        q, k, v, u = in_projection(modulate(xp, g_norm1[l], sh1, sc1), w_in[l])
        attn = context_attention(q, k, v, attn_sink[l])
        ssm, fin_re, fin_im = s5_bidirectional(u, ssm_l, None, None, True)
        xp = xp + gt1[:, None, :] * merge_groups(attn, ssm, g_out_attn[l], g_out_ssm[l], w_out[l])
        h = modulate(xp, g_norm2[l], sh2, sc2)
        xp = xp + gt2[:, None, :] * conv_ffn(h, w_up[l], conv_w[l], conv_b[l], w_down[l])
        new_k.append(k)
        new_v.append(v)
        new_s_re.append(fin_re)
        new_s_im.append(fin_im)

        sh1, sc1, gt1, sh2, sc2, gt2 = adaln(c, w_ada[l], b_ada[l])
        q, k, v, u = in_projection(modulate(xs, g_norm1[l], sh1, sc1), w_in[l])
        q = apply_axial_rope(q, ang_row, ang_col)
        k = apply_axial_rope(k, ang_row, ang_col)
        attn = latent_attention(q, k, v, cache_k[:, l], cache_v[:, l], attn_sink[l])
        ssm = s5_bidirectional(u, ssm_l, state_ssm_re[:, l], state_ssm_im[:, l], False)
        xs = xs + gt1[:, None, :] * merge_groups(attn, ssm, g_out_attn[l], g_out_ssm[l], w_out[l])
        h = modulate(xs, g_norm2[l], sh2, sc2)
        xs = xs + gt2[:, None, :] * conv_ffn(h, w_up[l], conv_w[l], conv_b[l], w_down[l])

    y_prompt = rms_norm(xp, g_final)
    y_sample = rms_norm(xs, g_final)
    return (y_prompt, y_sample, jnp.stack(new_k, axis=1), jnp.stack(new_v, axis=1),
            jnp.stack(new_s_re, axis=1), jnp.stack(new_s_im, axis=1))
```

```python
import functools
import math

import jax
import jax.numpy as jnp
from jax import lax
from jax.experimental import pallas as pl
from jax.experimental.pallas import tpu as pltpu

D_MODEL = 1024
HEAD_DIM = 64
N_HEADS = 8
KV_HEADS = 2
ATTN_WIDTH = N_HEADS * HEAD_DIM
KV_WIDTH = KV_HEADS * HEAD_DIM
BLOCK = 128
GRID_W = 64
ROPE_THETA = 10000.0
ROPE_FREQS = HEAD_DIM // 4
SSM_CH = 16
SSM_WIDTH = D_MODEL - ATTN_WIDTH
SSM_GROUPS = SSM_WIDTH // SSM_CH
SSM_STATE = 64
D_FF = 2816
EPS = 1e-6
NEG_INF = -1e30

LANES = 128
SUBLANES = 8
GROUPS_PER_BLOCK = LANES // SSM_CH
N_GROUP_BLOCKS = SSM_GROUPS // GROUPS_PER_BLOCK
STATE_LANES = GROUPS_PER_BLOCK * SSM_STATE
FF_CHUNK = 256
VMEM_LIMIT = 56 * 1024 * 1024

BF = jnp.bfloat16
F32 = jnp.float32


def _cparams(sem):
    return pltpu.CompilerParams(dimension_semantics=sem, vmem_limit_bytes=VMEM_LIMIT)


def _const_spec(shape):
    nd = len(shape)
    return pl.BlockSpec(shape, lambda *_: (0,) * nd, pipeline_mode=pl.Buffered(1))


def _rms(x):
    return x * lax.rsqrt(jnp.mean(x * x, axis=-1, keepdims=True) + EPS)


def _dot(a, b):
    return jnp.dot(a, b, preferred_element_type=F32)


def _dot_nt(a, b):
    return lax.dot_general(a, b, (((1,), (1,)), ((), ())), preferred_element_type=F32)


def _ada_kernel(cond_ref, w_ref, b_ref, o_ref):
    c = cond_ref[...]
    s = c * jax.nn.sigmoid(c)
    o_ref[...] = _dot(s.astype(BF), w_ref[...].astype(BF)) + b_ref[...]


def _adaln(cond, w_ada, b_ada):
    n = cond.shape[0]
    return pl.pallas_call(
        _ada_kernel,
        out_shape=jax.ShapeDtypeStruct((n, 6 * D_MODEL), F32),
        grid=(6,),
        in_specs=[pl.BlockSpec((n, D_MODEL), lambda i: (0, 0)),
                  pl.BlockSpec((D_MODEL, D_MODEL), lambda i: (0, i)),
                  pl.BlockSpec((1, D_MODEL), lambda i: (0, i))],
        out_specs=pl.BlockSpec((n, D_MODEL), lambda i: (0, i)),
        compiler_params=_cparams(("arbitrary",)),
        name="adaln",
    )(cond, w_ada, b_ada.reshape(1, -1))


def _rope(z, cos, sin_lo, sin_hi):
    return z * cos + pltpu.roll(z, LANES - ROPE_FREQS, 1) * sin_lo + pltpu.roll(z, ROPE_FREQS, 1) * sin_hi


def _inproj_kernel(*refs, rope, with_kv):
    x_ref, sh_ref, sc_ref, g_ref, w_ref = refs[:5]
    pos = 5
    if rope:
        cos_ref, slo_ref, shi_ref = refs[pos:pos + 3]
        pos += 3
    q_ref, kk_ref, vv_ref, u_ref = refs[pos:pos + 4]
    pos += 4
    h = _rms(x_ref[0]) * g_ref[...]
    h = h * (1.0 + sc_ref[0]) + sh_ref[0]
    proj = _dot(h.astype(BF), w_ref[...])
    q = proj[:, :ATTN_WIDTH] * (HEAD_DIM ** -0.5)
    kk = proj[:, ATTN_WIDTH:ATTN_WIDTH + 2 * KV_WIDTH]
    vv = proj[:, ATTN_WIDTH + 2 * KV_WIDTH:ATTN_WIDTH + 4 * KV_WIDTH]
    u0 = ATTN_WIDTH + 4 * KV_WIDTH
    u_ref[0] = proj[:, u0:u0 + SSM_WIDTH]
    vv_ref[0] = vv.astype(BF)
    if rope:
        cos, slo, shi = cos_ref[...], slo_ref[...], shi_ref[...]
        for c in range(ATTN_WIDTH // LANES):
            q_ref[0, :, c * LANES:(c + 1) * LANES] = _rope(
                q[:, c * LANES:(c + 1) * LANES], cos, slo, shi).astype(BF)
        for c in range(2 * KV_WIDTH // LANES):
            kk_ref[0, :, c * LANES:(c + 1) * LANES] = _rope(
                kk[:, c * LANES:(c + 1) * LANES], cos, slo, shi).astype(BF)
    else:
        q_ref[0] = q.astype(BF)
        kk_ref[0] = kk.astype(BF)
    if with_kv:
        k_ref, v_ref = refs[pos:pos + 2]
        k0 = u0 + SSM_WIDTH
        k_ref[0] = proj[:, k0:k0 + KV_WIDTH]
        v_ref[0] = proj[:, k0 + KV_WIDTH:k0 + 2 * KV_WIDTH]


def _in_projection(x, shift, scale, g, w, rope_tabs, with_kv, tile):
    b, n, _ = x.shape
    nc = w.shape[1]
    per_batch = shift.shape[0] > 1
    mod_spec = pl.BlockSpec((1, 1, D_MODEL), (lambda i, t: (i, 0, 0)) if per_batch else (lambda i, t: (0, 0, 0)))
    in_specs = [pl.BlockSpec((1, tile, D_MODEL), lambda i, t: (i, t, 0)),
                mod_spec, mod_spec,
                _const_spec((1, D_MODEL)),
                _const_spec((D_MODEL, nc))]
    args = [x, shift, scale, g.reshape(1, -1), w]
    if rope_tabs is not None:
        in_specs += [pl.BlockSpec((tile, LANES), lambda i, t: (t, 0))] * 3
        args += list(rope_tabs)

    def out(width, dtype):
        return (jax.ShapeDtypeStruct((b, n, width), dtype),
                pl.BlockSpec((1, tile, width), lambda i, t: (i, t, 0)))

    outs = [out(ATTN_WIDTH, BF), out(2 * KV_WIDTH, BF), out(2 * KV_WIDTH, BF), out(SSM_WIDTH, F32)]
    if with_kv:
        outs += [out(KV_WIDTH, F32), out(KV_WIDTH, F32)]
    return pl.pallas_call(
        functools.partial(_inproj_kernel, rope=rope_tabs is not None, with_kv=with_kv),
        out_shape=[o[0] for o in outs],
        grid=(b, n // tile),
        in_specs=in_specs,
        out_specs=[o[1] for o in outs],
        compiler_params=_cparams(("arbitrary", "arbitrary")),
        name="in_projection",
    )(*args)


def _rope_tables(n_tokens):
    t = jnp.arange(n_tokens, dtype=F32)
    row = jnp.floor(t / GRID_W)
    col = t - row * GRID_W
    inv_freq = ROPE_THETA ** (-jnp.arange(ROPE_FREQS, dtype=F32) / ROPE_FREQS)
    ang_row = row[:, None] * inv_freq[None, :]
    ang_col = col[:, None] * inv_freq[None, :]
    ang = jnp.concatenate([ang_row, ang_row, ang_col, ang_col] * (LANES // HEAD_DIM), axis=-1)
    first = (jnp.arange(LANES) % (2 * ROPE_FREQS)) < ROPE_FREQS
    sin = jnp.sin(ang)
    return jnp.cos(ang), jnp.where(first, -sin, 0.0), jnp.where(first, 0.0, sin)


def _stack_heads(q_rows, j, lo):
    qa = q_rows[:, (2 * j) * LANES:(2 * j + 1) * LANES]
    qb = q_rows[:, (2 * j + 1) * LANES:(2 * j + 2) * LANES]
    keep_lo = jnp.where(lo, 1.0, 0.0).astype(BF)
    keep_hi = jnp.where(lo, 0.0, 1.0).astype(BF)
    return jnp.concatenate([qa * keep_lo, qa * keep_hi, qb * keep_lo, qb * keep_hi], axis=0)


def _sink_column(sink_ref, j, t):
    return jnp.concatenate([jnp.full((t, 1), sink_ref[4 * j + a], F32) for a in range(4)], axis=0)


def _softmax_pv(scores, values, sink_col):
    m = sink_col
    for s in scores:
        m = jnp.maximum(m, jnp.max(s, axis=-1, keepdims=True))
    den = jnp.exp(sink_col - m)
    acc = None
    for s, v in zip(scores, values):
        p = jnp.exp(s - m)
        den = den + jnp.sum(p, axis=-1, keepdims=True)
        o = _dot(p.astype(BF), v)
        acc = o if acc is None else acc + o
    return acc / den


def _unstack_store(o_ref, o, rows, j, lo, t):
    o_ref[0, rows, (2 * j) * LANES:(2 * j + 1) * LANES] = jnp.where(lo, o[0:t], o[t:2 * t])
    o_ref[0, rows, (2 * j + 1) * LANES:(2 * j + 2) * LANES] = jnp.where(lo, o[2 * t:3 * t], o[3 * t:4 * t])


def _ctx_attn_kernel(sink_ref, q_ref, kk_ref, vv_ref, o_ref, *, n):
    lo = lax.broadcasted_iota(jnp.int32, (BLOCK, LANES), 1) < HEAD_DIM
    for sub in range(n // BLOCK):
        rows = slice(sub * BLOCK, (sub + 1) * BLOCK)
        q_rows = q_ref[0, rows, :]
        for j in range(KV_HEADS):
            cols = slice(j * LANES, (j + 1) * LANES)
            qs = _stack_heads(q_rows, j, lo)
            s = _dot_nt(qs, kk_ref[0, :, cols])
            o = _softmax_pv([s], [vv_ref[0, :, cols]], _sink_column(sink_ref, j, BLOCK))
            _unstack_store(o_ref, o, rows, j, lo, BLOCK)


def _context_attention(q, kk, vv, sink):
    b, n, _ = q.shape
    return pl.pallas_call(
        functools.partial(_ctx_attn_kernel, n=n),
        out_shape=jax.ShapeDtypeStruct((b, n, ATTN_WIDTH), F32),
        grid=(b,),
        in_specs=[pl.BlockSpec(memory_space=pltpu.SMEM),
                  pl.BlockSpec((1, n, ATTN_WIDTH), lambda i: (i, 0, 0)),
                  pl.BlockSpec((1, n, 2 * KV_WIDTH), lambda i: (i, 0, 0)),
                  pl.BlockSpec((1, n, 2 * KV_WIDTH), lambda i: (i, 0, 0))],
        out_specs=pl.BlockSpec((1, n, ATTN_WIDTH), lambda i: (i, 0, 0)),
        compiler_params=_cparams(("arbitrary",)),
        name="context_attention",
    )(sink, q, kk, vv)


def _lat_attn_kernel(sink_ref, q_ref, kkm_ref, kkp_ref, kkn_ref, vvm_ref, vvp_ref, vvn_ref,
                     ckk_ref, cvv_ref, o_ref, kext, vext, *, tq, n):
    t = pl.program_id(1)
    kext[0:BLOCK] = kkp_ref[0]
    kext[BLOCK:BLOCK + tq] = kkm_ref[0]
    kext[BLOCK + tq:2 * BLOCK + tq] = kkn_ref[0]
    vext[0:BLOCK] = vvp_ref[0]
    vext[BLOCK:BLOCK + tq] = vvm_ref[0]
    vext[BLOCK + tq:2 * BLOCK + tq] = vvn_ref[0]
    lo = lax.broadcasted_iota(jnp.int32, (BLOCK, LANES), 1) < HEAD_DIM
    r = lax.broadcasted_iota(jnp.int32, (BLOCK, 3 * BLOCK), 0)
    w = lax.broadcasted_iota(jnp.int32, (BLOCK, 3 * BLOCK), 1)
    band = (w >= r) & (w <= r + 2 * BLOCK)
    for sub in range(tq // BLOCK):
        rows = slice(sub * BLOCK, (sub + 1) * BLOCK)
        key = t * tq + (sub - 1) * BLOCK + w
        valid = jnp.where(band & (key >= 0) & (key < n), 1.0, 0.0)
        valid4 = jnp.concatenate([valid] * 4, axis=0) > 0.5
        q_rows = q_ref[0, rows, :]
        win = slice(sub * BLOCK, (sub + 3) * BLOCK)
        for j in range(KV_HEADS):
            cols = slice(j * LANES, (j + 1) * LANES)
            qs = _stack_heads(q_rows, j, lo)
            s_loc = jnp.where(valid4, _dot_nt(qs, kext[win, cols]), NEG_INF)
            s_ctx = _dot_nt(qs, ckk_ref[0, :, cols])
            o = _softmax_pv([s_loc, s_ctx], [vext[win, cols], cvv_ref[0, :, cols]],
                            _sink_column(sink_ref, j, BLOCK))
            _unstack_store(o_ref, o, rows, j, lo, BLOCK)


def _latent_attention(q, kk, vv, ckk, cvv, sink, tq):
    b, n, _ = q.shape
    past = ckk.shape[1]
    nblk = n // BLOCK
    per = tq // BLOCK
    kw = 2 * KV_WIDTH
    main = pl.BlockSpec((1, tq, kw), lambda i, t: (i, t, 0))
    prev = pl.BlockSpec((1, BLOCK, kw), lambda i, t: (i, jnp.maximum(t * per - 1, 0), 0))
    nxt = pl.BlockSpec((1, BLOCK, kw), lambda i, t: (i, jnp.minimum((t + 1) * per, nblk - 1), 0))
    ctx = pl.BlockSpec((1, past, kw), lambda i, t: (i, 0, 0))
    return pl.pallas_call(
        functools.partial(_lat_attn_kernel, tq=tq, n=n),
        out_shape=jax.ShapeDtypeStruct((b, n, ATTN_WIDTH), F32),
        grid=(b, n // tq),
        in_specs=[pl.BlockSpec(memory_space=pltpu.SMEM),
                  pl.BlockSpec((1, tq, ATTN_WIDTH), lambda i, t: (i, t, 0)),
                  main, prev, nxt, main, prev, nxt, ctx, ctx],
        out_specs=pl.BlockSpec((1, tq, ATTN_WIDTH), lambda i, t: (i, t, 0)),
        scratch_shapes=[pltpu.VMEM((tq + 2 * BLOCK, kw), BF), pltpu.VMEM((tq + 2 * BLOCK, kw), BF)],
        compiler_params=_cparams(("arbitrary", "arbitrary")),
        name="latent_attention",
    )(sink, q, kk, kk, kk, vv, vv, vv, ckk, cvv)


def _ssm_prepare(lam_re, lam_im, log_dt, b_re, b_im, c_re, c_im, d, w_glu, ntap):
    lam_re, lam_im = lam_re.astype(F32), lam_im.astype(F32)
    dt = jnp.exp(log_dt.astype(F32))[..., None]
    mag = jnp.exp(lam_re * dt)
    ang = lam_im * dt
    a_re, a_im = mag * jnp.cos(ang), mag * jnp.sin(ang)
    den = lam_re * lam_re + lam_im * lam_im
    num_re = a_re - 1.0
    coef_re = (num_re * lam_re + a_im * lam_im) / den
    coef_im = (a_im * lam_re - num_re * lam_im) / den
    br, bi = b_re.astype(F32), b_im.astype(F32)
    t_re = coef_re[..., None] * br - coef_im[..., None] * bi
    t_im = coef_re[..., None] * bi + coef_im[..., None] * br
    eye = jnp.eye(GROUPS_PER_BLOCK, dtype=F32)

    def block_diag_in(t):
        t = t.reshape(2, N_GROUP_BLOCKS, GROUPS_PER_BLOCK, SSM_STATE, SSM_CH)
        return jnp.einsum("dbgsp,gh->dbgphs", t, eye).reshape(2, N_GROUP_BLOCKS, LANES, STATE_LANES)

    taps = []
    p_re, p_im = jnp.ones_like(a_re), jnp.zeros_like(a_re)
    for _ in range(ntap):
        k_re = p_re[..., None] * t_re - p_im[..., None] * t_im
        k_im = p_re[..., None] * t_im + p_im[..., None] * t_re
        taps.append(jnp.concatenate([block_diag_in(k_re), block_diag_in(k_im)], axis=-1))
        p_re, p_im = p_re * a_re - p_im * a_im, p_re * a_im + p_im * a_re
    w_b = jnp.concatenate(taps, axis=2).astype(BF)

    def block_diag_out(c):
        c = c.astype(F32).reshape(2, N_GROUP_BLOCKS, GROUPS_PER_BLOCK, SSM_CH, SSM_STATE)
        return jnp.einsum("dbgps,gh->dbgshp", c, eye).reshape(2, N_GROUP_BLOCKS, STATE_LANES, LANES)

    w_c = jnp.concatenate([block_diag_out(c_re), -block_diag_out(c_im)], axis=2).astype(BF)

    def lanes(z):
        return z.reshape(2, N_GROUP_BLOCKS, 1, STATE_LANES)

    coef = jnp.concatenate([lanes(p_re), lanes(p_im), lanes(a_re), lanes(a_im)], axis=2)
    wg = w_glu.astype(F32).reshape(N_GROUP_BLOCKS, GROUPS_PER_BLOCK, SSM_CH, SSM_CH)
    w_g = jnp.einsum("bgpq,gh->bgphq", wg, eye).reshape(N_GROUP_BLOCKS, LANES, LANES).astype(BF)
    return w_b, w_c, coef, d.astype(F32).reshape(1, SSM_WIDTH), w_g


def _ssm_kernel(u_ref, wb_ref, wc_ref, coef_ref, h0re_ref, h0im_ref, d_ref, wg_ref,
                y_ref, fre_ref, fim_ref, ybwd, ubuf, bu, vre, vim, *, rows, step_rows, bsz, ntap, n_tiles):
    phase = pl.program_id(1)
    i = pl.program_id(2)
    n_steps = rows // step_rows
    unroll = 8

    def run(forward):
        tile = i if forward else n_tiles - 1 - i
        coef = coef_ref[0, 0]
        as_re, as_im = coef[0:1], coef[1:2]
        row = lax.broadcasted_iota(jnp.int32, (step_rows, STATE_LANES), 0)
        if ntap > 1:
            newer = (row >= bsz) if forward else (row < bsz)
        u = u_ref[...]

        @pl.when(i == 0)
        def _():
            h0r, h0i = h0re_ref[0, 0], h0im_ref[0, 0]
            if ntap > 1:
                vre[...] = jnp.where(newer, h0r, 0.0)
                vim[...] = jnp.where(newer, h0i, 0.0)
                ubuf[...] = jnp.zeros_like(ubuf)
            else:
                vre[...] = h0r
                vim[...] = h0i

        if ntap > 1:
            if forward:
                ubuf[SUBLANES:SUBLANES + rows] = u
            else:
                ubuf[0:rows] = u
            shifted = ubuf[bsz:bsz + rows]
            if forward:
                ubuf[0:SUBLANES] = u[rows - SUBLANES:rows]
            else:
                ubuf[rows:rows + SUBLANES] = u[0:SUBLANES]
            lhs = jnp.concatenate([u, shifted], axis=1)
        else:
            lhs = u
        bu[...] = _dot(lhs.astype(BF), wb_ref[0, 0])

        if ntap > 1:
            @pl.when(i == 0)
            def _():
                h0r, h0i = h0re_ref[0, 0], h0im_ref[0, 0]
                a_re, a_im = coef[2:3], coef[3:4]
                e_re = jnp.where(newer, 0.0, a_re * h0r - a_im * h0i)
                e_im = jnp.where(newer, 0.0, a_re * h0i + a_im * h0r)
                first = slice(0, step_rows) if forward else slice(rows - step_rows, rows)
                bu[first, 0:STATE_LANES] += e_re
                bu[first, STATE_LANES:2 * STATE_LANES] += e_im

        ar = jnp.broadcast_to(as_re, (step_rows, STATE_LANES))
        ai = jnp.broadcast_to(as_im, (step_rows, STATE_LANES))

        def body(k, carry):
            v_r, v_i = carry
            for uu in range(unroll):
                s = k * unroll + uu
                s = s if forward else n_steps - 1 - s
                off = pl.multiple_of(s * step_rows, step_rows)
                b_r = bu[pl.ds(off, step_rows), 0:STATE_LANES]
                b_i = bu[pl.ds(off, step_rows), STATE_LANES:2 * STATE_LANES]
                n_r = ar * v_r - ai * v_i + b_r
                n_i = ar * v_i + ai * v_r + b_i
                bu[pl.ds(off, step_rows), 0:STATE_LANES] = n_r
                bu[pl.ds(off, step_rows), STATE_LANES:2 * STATE_LANES] = n_i
                v_r, v_i = n_r, n_i
            return v_r, v_i

        v_r, v_i = lax.fori_loop(0, n_steps // unroll, body, (vre[...], vim[...]))
        vre[...] = v_r
        vim[...] = v_i

        @pl.when(i == n_tiles - 1)
        def _():
            fre_ref[0, 0] = v_r
            fim_ref[0, 0] = v_i

        y_part = _dot(bu[...].astype(BF), wc_ref[0, 0])
        base = pl.multiple_of(tile * rows, rows)
        if forward:
            y = ybwd[pl.ds(base, rows), :] + y_part + u * d_ref[...]
            x = jax.nn.gelu(y, approximate=True)
            y_ref[...] = x * jax.nn.sigmoid(_dot(x.astype(BF), wg_ref[0]))
        else:
            ybwd[pl.ds(base, rows), :] = y_part

    @pl.when(phase == 0)
    def _():
        run(False)

    @pl.when(phase == 1)
    def _():
        run(True)


def _s5_scan(u_tm, prep, h0_re, h0_im, bsz, ntap, rows):
    w_b, w_c, coef, d_vec, w_g = prep
    total = u_tm.shape[0]
    n_tiles = total // rows
    step_rows = bsz * ntap
    kb = LANES * ntap

    def tile_of(p, i):
        return jnp.where(p == 0, n_tiles - 1 - i, i)

    dir_spec = lambda shape: pl.BlockSpec((1, 1) + shape, lambda g, p, i: (1 - p, g, 0, 0))
    state_shape = jax.ShapeDtypeStruct((2, N_GROUP_BLOCKS, step_rows, STATE_LANES), F32)
    return pl.pallas_call(
        functools.partial(_ssm_kernel, rows=rows, step_rows=step_rows, bsz=bsz, ntap=ntap, n_tiles=n_tiles),
        out_shape=[jax.ShapeDtypeStruct((total, SSM_WIDTH), F32), state_shape, state_shape],
        grid=(N_GROUP_BLOCKS, 2, n_tiles),
        in_specs=[pl.BlockSpec((rows, LANES), lambda g, p, i: (tile_of(p, i), g)),
                  dir_spec((kb, 2 * STATE_LANES)),
                  dir_spec((2 * STATE_LANES, LANES)),
                  dir_spec((4, STATE_LANES)),
                  dir_spec((step_rows, STATE_LANES)),
                  dir_spec((step_rows, STATE_LANES)),
                  pl.BlockSpec((1, LANES), lambda g, p, i: (0, g)),
                  pl.BlockSpec((1, LANES, LANES), lambda g, p, i: (g, 0, 0))],
        out_specs=[pl.BlockSpec((rows, LANES), lambda g, p, i: (p * i, g)),
                   dir_spec((step_rows, STATE_LANES)),
                   dir_spec((step_rows, STATE_LANES))],
        scratch_shapes=[pltpu.VMEM((total, LANES), F32),
                        pltpu.VMEM((rows + SUBLANES, LANES), F32),
                        pltpu.VMEM((rows, 2 * STATE_LANES), F32),
                        pltpu.VMEM((step_rows, STATE_LANES), F32),
                        pltpu.VMEM((step_rows, STATE_LANES), F32)],
        compiler_params=_cparams(("arbitrary", "arbitrary", "arbitrary")),
        name="s5_scan",
    )(u_tm, w_b, w_c, coef, h0_re, h0_im, d_vec, w_g)


def _merge_kernel(attn_ref, ssm_ref, x_ref, gt_ref, ga_ref, gs_ref, w_ref, o_ref):
    na = _rms(attn_ref[0]) * ga_ref[...]
    ns = _rms(ssm_ref[0]) * gs_ref[...]
    mo = _dot(na.astype(BF), w_ref[0:ATTN_WIDTH, :]) + _dot(ns.astype(BF), w_ref[ATTN_WIDTH:D_MODEL, :])
    o_ref[0] = x_ref[0] + gt_ref[0] * mo


def _merge(attn, ssm, x, gate, g_attn, g_ssm, w_out, tile):
    b, n, _ = x.shape
    per_batch = gate.shape[0] > 1
    row = lambda width: pl.BlockSpec((1, tile, width), lambda i, t: (i, t, 0))
    return pl.pallas_call(
        _merge_kernel,
        out_shape=jax.ShapeDtypeStruct((b, n, D_MODEL), F32),
        grid=(b, n // tile),
        in_specs=[row(ATTN_WIDTH), row(SSM_WIDTH), row(D_MODEL),
                  pl.BlockSpec((1, 1, D_MODEL), (lambda i, t: (i, 0, 0)) if per_batch else (lambda i, t: (0, 0, 0))),
                  _const_spec((1, ATTN_WIDTH)), _const_spec((1, SSM_WIDTH)),
                  _const_spec((D_MODEL, D_MODEL))],
        out_specs=row(D_MODEL),
        compiler_params=_cparams(("arbitrary", "arbitrary")),
        name="merge_projection",
    )(attn, ssm, x, gate, g_attn.reshape(1, -1), g_ssm.reshape(1, -1), w_out)


HALO = 16


def _ffn_kernel(x_ref, xp_ref, xn_ref, sh_ref, sc_ref, gt_ref, g2_ref, wup_ref, cw_ref, cb_ref, wdn_ref, gf_ref,
                o_ref, hext, upbuf, act, *, tile, n_tiles):
    t = pl.program_id(1)

    def modulated(x):
        h = _rms(x) * g2_ref[...]
        return (h * (1.0 + sc_ref[0]) + sh_ref[0]).astype(BF)

    x = x_ref[0]
    hext[0:HALO] = jnp.where(t > 0, modulated(xp_ref[0]), jnp.zeros((HALO, D_MODEL), BF))
    hext[HALO:HALO + tile] = modulated(x)
    hext[HALO + tile:2 * HALO + tile] = jnp.where(t < n_tiles - 1, modulated(xn_ref[0]),
                                                  jnp.zeros((HALO, D_MODEL), BF))
    h_all = hext[...]
    for c in range(D_FF // FF_CHUNK):
        halves = []
        for base in (c * FF_CHUNK, D_FF + c * FF_CHUNK):
            cols = slice(base, base + FF_CHUNK)
            upbuf[...] = _dot(h_all, wup_ref[:, cols])
            conv = (upbuf[HALO - 1:HALO - 1 + tile] * cw_ref[0:1, cols]
                    + upbuf[HALO:HALO + tile] * cw_ref[1:2, cols]
                    + upbuf[HALO + 1:HALO + 1 + tile] * cw_ref[2:3, cols]
                    + cb_ref[:, cols])
            halves.append(conv)
        gate, val = halves
        act[:, c * FF_CHUNK:(c + 1) * FF_CHUNK] = (gate * jax.nn.sigmoid(gate) * val).astype(BF)
    y = x + gt_ref[0] * _dot(act[...], wdn_ref[...])
    o_ref[0] = _rms(y) * gf_ref[...]


def _conv_ffn(x, shift, scale, gate, g2, w_up, conv_w, conv_b, w_down, g_final, tile):
    b, n, _ = x.shape
    n_tiles = n // tile
    per = tile // HALO
    nh = n // HALO
    per_batch = shift.shape[0] > 1
    mod_spec = pl.BlockSpec((1, 1, D_MODEL), (lambda i, t: (i, 0, 0)) if per_batch else (lambda i, t: (0, 0, 0)))
    return pl.pallas_call(
        functools.partial(_ffn_kernel, tile=tile, n_tiles=n_tiles),
        out_shape=jax.ShapeDtypeStruct((b, n, D_MODEL), F32),
        grid=(b, n_tiles),
        in_specs=[pl.BlockSpec((1, tile, D_MODEL), lambda i, t: (i, t, 0)),
                  pl.BlockSpec((1, HALO, D_MODEL), lambda i, t: (i, jnp.maximum(t * per - 1, 0), 0)),
                  pl.BlockSpec((1, HALO, D_MODEL), lambda i, t: (i, jnp.minimum((t + 1) * per, nh - 1), 0)),
                  mod_spec, mod_spec, mod_spec,
                  _const_spec((1, D_MODEL)),
                  _const_spec((D_MODEL, 2 * D_FF)),
                  _const_spec((3, 2 * D_FF)),
                  _const_spec((1, 2 * D_FF)),
                  _const_spec((D_FF, D_MODEL)),
                  _const_spec((1, D_MODEL))],
        out_specs=pl.BlockSpec((1, tile, D_MODEL), lambda i, t: (i, t, 0)),
        scratch_shapes=[pltpu.VMEM((tile + 2 * HALO, D_MODEL), BF),
                        pltpu.VMEM((tile + 2 * HALO, FF_CHUNK), F32),
                        pltpu.VMEM((tile, D_FF), BF)],
        compiler_params=_cparams(("arbitrary", "arbitrary")),
        name="conv_ffn",
    )(x, x, x, shift, scale, gate, g2.reshape(1, -1), w_up, conv_w, conv_b.reshape(1, -1), w_down,
      g_final.reshape(1, -1))


def _dup_heads(w):
    a, b = w[..., :HEAD_DIM], w[..., HEAD_DIM:]
    return jnp.concatenate([a, a, b, b], axis=-1)


def _stream(x, mods, weights, rope_tabs, ctx_kv, h0, ssm_prep, with_kv, tile, ssm_rows, attn_tile):
    sh1, sc1, gt1, sh2, sc2, gt2 = mods
    (g1, g2, w_in, sink, g_attn, g_ssm, w_out, w_up, conv_w, conv_b, w_down, g_final) = weights
    b, n, _ = x.shape
    outs = _in_projection(x, sh1, sc1, g1, w_in, rope_tabs, with_kv, tile)
    q, kk, vv, u = outs[:4]
    if ctx_kv is None:
        attn = _context_attention(q, kk, vv, sink)
    else:
        attn = _latent_attention(q, kk, vv, ctx_kv[0], ctx_kv[1], sink, attn_tile)
    ntap = max(1, SUBLANES // b)
    u_tm = u.transpose(1, 0, 2).reshape(n * b, SSM_WIDTH)
    y_tm, fin_re, fin_im = _s5_scan(u_tm, ssm_prep[ntap], h0[0], h0[1], b, ntap, ssm_rows)
    ssm = y_tm.reshape(n, b, SSM_WIDTH).transpose(1, 0, 2)
    x1 = _merge(attn, ssm, x, gt1, g_attn, g_ssm, w_out, tile)
    y = _conv_ffn(x1, sh2, sc2, gt2, g2, w_up, conv_w, conv_b, w_down, g_final, tile)
    return y, outs[4:], (fin_re, fin_im)


def kernel(x_prompt, x_sample, cache_k, cache_v, state_ssm_re, state_ssm_im, c, c_ctx, w_ada, b_ada, g_norm1, g_norm2, w_in, attn_sink, ssm_lam_re, ssm_lam_im, ssm_log_dt, ssm_b_re, ssm_b_im, ssm_c_re, ssm_c_im, ssm_d, ssm_w_glu, g_out_attn, g_out_ssm, w_out, w_up, conv_w, conv_b, w_down, g_final):
    depth = w_in.shape[0]
    assert depth == 1
    l = 0
    bp, seq, _ = x_prompt.shape
    bs, dec_seq, _ = x_sample.shape
    past = cache_k.shape[2]

    cond = jnp.concatenate([c_ctx[None, :], c, jnp.zeros((SUBLANES - 1 - bs, D_MODEL), F32)], axis=0)
    mod = _adaln(cond, w_ada[l], b_ada[l])
    mods_ctx = [mod[0:1, i * D_MODEL:(i + 1) * D_MODEL].reshape(1, 1, D_MODEL) for i in range(6)]
    mods_lat = [mod[1:1 + bs, i * D_MODEL:(i + 1) * D_MODEL].reshape(bs, 1, D_MODEL) for i in range(6)]

    wi = w_in[l]
    wq = wi[:, :ATTN_WIDTH]
    wk = wi[:, ATTN_WIDTH:ATTN_WIDTH + KV_WIDTH]
    wv = wi[:, ATTN_WIDTH + KV_WIDTH:ATTN_WIDTH + 2 * KV_WIDTH]
    wu = wi[:, ATTN_WIDTH + 2 * KV_WIDTH:]
    w_in_lat = jnp.concatenate([wq, _dup_heads(wk), _dup_heads(wv), wu], axis=1).astype(BF)
    w_in_ctx = jnp.concatenate([w_in_lat, wk.astype(BF), wv.astype(BF)], axis=1)
    weights = (g_norm1[l], g_norm2[l], None, attn_sink[l], g_out_attn[l], g_out_ssm[l], w_out[l].astype(BF),
               w_up[l].astype(BF), conv_w[l], conv_b[l], w_down[l].astype(BF), g_final)

    ssm_params = (ssm_lam_re[l], ssm_lam_im[l], ssm_log_dt[l], ssm_b_re[l], ssm_b_im[l],
                  ssm_c_re[l], ssm_c_im[l], ssm_d[l], ssm_w_glu[l])
    taps = {max(1, SUBLANES // bp), max(1, SUBLANES // bs)}
    ssm_prep = {nt: _ssm_prepare(*ssm_params, nt) for nt in taps}

    zeros_h0 = jnp.zeros((2, N_GROUP_BLOCKS, bp, STATE_LANES), F32)
    w_ctx = weights[:2] + (w_in_ctx,) + weights[3:]
    y_prompt, (k_new, v_new), (fin_re, fin_im) = _stream(
        x_prompt, mods_ctx, w_ctx, None, None, (zeros_h0, zeros_h0), ssm_prep, True,
        tile=seq, ssm_rows=1024, attn_tile=None)
    new_k = k_new.reshape(bp, 1, seq, KV_HEADS, HEAD_DIM)
    new_v = v_new.reshape(bp, 1, seq, KV_HEADS, HEAD_DIM)

    def states_out(fin):
        return fin.transpose(2, 0, 1, 3).reshape(bp, 1, 2, SSM_GROUPS, SSM_STATE)

    ckk = _dup_heads(cache_k[:, l].reshape(bs, past, KV_WIDTH)).astype(BF)
    cvv = _dup_heads(cache_v[:, l].reshape(bs, past, KV_WIDTH)).astype(BF)

    def h0_in(state):
        s = state.astype(F32).reshape(bs, 2, N_GROUP_BLOCKS, STATE_LANES).transpose(1, 2, 0, 3)
        return jnp.concatenate([s] * max(1, SUBLANES // bs), axis=2)

    w_lat = weights[:2] + (w_in_lat,) + weights[3:]
    y_sample, _, _ = _stream(
        x_sample, mods_lat, w_lat, _rope_tables(dec_seq), (ckk, cvv),
        (h0_in(state_ssm_re[:, l]), h0_in(state_ssm_im[:, l])), ssm_prep, False,
        tile=512, ssm_rows=1024, attn_tile=512)

    return (y_prompt, y_sample, new_k, new_v, states_out(fin_re), states_out(fin_im))
```

```python
import functools
import math

import jax
import jax.numpy as jnp
from jax import lax
from jax.experimental import pallas as pl
from jax.experimental.pallas import tpu as pltpu

D_MODEL = 1024
HEAD_DIM = 64
N_HEADS = 8
KV_HEADS = 2
ATTN_WIDTH = N_HEADS * HEAD_DIM
KV_WIDTH = KV_HEADS * HEAD_DIM
BLOCK = 128
GRID_W = 64
ROPE_THETA = 10000.0
ROPE_FREQS = HEAD_DIM // 4
SSM_CH = 16
SSM_WIDTH = D_MODEL - ATTN_WIDTH
SSM_GROUPS = SSM_WIDTH // SSM_CH
SSM_STATE = 64
D_FF = 2816
EPS = 1e-6
NEG_INF = -1e30

LANES = 128
SUBLANES = 8
GROUPS_PER_BLOCK = LANES // SSM_CH
N_GROUP_BLOCKS = SSM_GROUPS // GROUPS_PER_BLOCK
STATE_LANES = GROUPS_PER_BLOCK * SSM_STATE
FF_CHUNK = 256
VMEM_LIMIT = 56 * 1024 * 1024

BF = jnp.bfloat16
F32 = jnp.float32


def _cparams(sem):
    return pltpu.CompilerParams(dimension_semantics=sem, vmem_limit_bytes=VMEM_LIMIT)


def _const_spec(shape):
    nd = len(shape)
    return pl.BlockSpec(shape, lambda *_: (0,) * nd, pipeline_mode=pl.Buffered(1))


def _rms(x):
    return x * lax.rsqrt(jnp.mean(x * x, axis=-1, keepdims=True) + EPS)


def _dot(a, b):
    return jnp.dot(a, b, preferred_element_type=F32)


def _dot_nt(a, b):
    return lax.dot_general(a, b, (((1,), (1,)), ((), ())), preferred_element_type=F32)


def _ada_kernel(cond_ref, w_ref, b_ref, o_ref):
    c = cond_ref[...]
    s = c * jax.nn.sigmoid(c)
    o_ref[...] = _dot(s.astype(BF), w_ref[...].astype(BF)) + b_ref[...]


def _adaln(cond, w_ada, b_ada):
    n = cond.shape[0]
    return pl.pallas_call(
        _ada_kernel,
        out_shape=jax.ShapeDtypeStruct((n, 6 * D_MODEL), F32),
        grid=(6,),
        in_specs=[pl.BlockSpec((n, D_MODEL), lambda i: (0, 0)),
                  pl.BlockSpec((D_MODEL, D_MODEL), lambda i: (0, i)),
                  pl.BlockSpec((1, D_MODEL), lambda i: (0, i))],
        out_specs=pl.BlockSpec((n, D_MODEL), lambda i: (0, i)),
        compiler_params=_cparams(("arbitrary",)),
        name="adaln",
    )(cond, w_ada, b_ada.reshape(1, -1))


def _rope(z, cos, sin_lo, sin_hi):
    return z * cos + pltpu.roll(z, LANES - ROPE_FREQS, 1) * sin_lo + pltpu.roll(z, ROPE_FREQS, 1) * sin_hi


def _inproj_kernel(*refs, rope, with_kv):
    x_ref, sh_ref, sc_ref, g_ref, w_ref = refs[:5]
    pos = 5
    if rope:
        cos_ref, slo_ref, shi_ref = refs[pos:pos + 3]
        pos += 3
    q_ref, kk_ref, vv_ref, u_ref = refs[pos:pos + 4]
    pos += 4
    h = _rms(x_ref[0]) * g_ref[...]
    h = h * (1.0 + sc_ref[0]) + sh_ref[0]
    proj = _dot(h.astype(BF), w_ref[...])
    q = proj[:, :ATTN_WIDTH] * (HEAD_DIM ** -0.5)
    kk = proj[:, ATTN_WIDTH:ATTN_WIDTH + 2 * KV_WIDTH]
    vv = proj[:, ATTN_WIDTH + 2 * KV_WIDTH:ATTN_WIDTH + 4 * KV_WIDTH]
    u0 = ATTN_WIDTH + 4 * KV_WIDTH
    u_ref[0] = proj[:, u0:u0 + SSM_WIDTH]
    vv_ref[0] = vv.astype(BF)
    if rope:
        cos, slo, shi = cos_ref[...], slo_ref[...], shi_ref[...]
        for c in range(ATTN_WIDTH // LANES):
            q_ref[0, :, c * LANES:(c + 1) * LANES] = _rope(
                q[:, c * LANES:(c + 1) * LANES], cos, slo, shi).astype(BF)
        for c in range(2 * KV_WIDTH // LANES):
            kk_ref[0, :, c * LANES:(c + 1) * LANES] = _rope(
                kk[:, c * LANES:(c + 1) * LANES], cos, slo, shi).astype(BF)
    else:
        q_ref[0] = q.astype(BF)
        kk_ref[0] = kk.astype(BF)
    if with_kv:
        k_ref, v_ref = refs[pos:pos + 2]
        k0 = u0 + SSM_WIDTH
        k_ref[0] = proj[:, k0:k0 + KV_WIDTH]
        v_ref[0] = proj[:, k0 + KV_WIDTH:k0 + 2 * KV_WIDTH]


def _in_projection(x, shift, scale, g, w, rope_tabs, with_kv, tile):
    b, n, _ = x.shape
    nc = w.shape[1]
    per_batch = shift.shape[0] > 1
    mod_spec = pl.BlockSpec((1, 1, D_MODEL), (lambda i, t: (i, 0, 0)) if per_batch else (lambda i, t: (0, 0, 0)))
    in_specs = [pl.BlockSpec((1, tile, D_MODEL), lambda i, t: (i, t, 0)),
                mod_spec, mod_spec,
                _const_spec((1, D_MODEL)),
                _const_spec((D_MODEL, nc))]
    args = [x, shift, scale, g.reshape(1, -1), w]
    if rope_tabs is not None:
        in_specs += [pl.BlockSpec((tile, LANES), lambda i, t: (t, 0))] * 3
        args += list(rope_tabs)

    def out(width, dtype):
        return (jax.ShapeDtypeStruct((b, n, width), dtype),
                pl.BlockSpec((1, tile, width), lambda i, t: (i, t, 0)))

    outs = [out(ATTN_WIDTH, BF), out(2 * KV_WIDTH, BF), out(2 * KV_WIDTH, BF), out(SSM_WIDTH, F32)]
    if with_kv:
        outs += [out(KV_WIDTH, F32), out(KV_WIDTH, F32)]
    return pl.pallas_call(
        functools.partial(_inproj_kernel, rope=rope_tabs is not None, with_kv=with_kv),
        out_shape=[o[0] for o in outs],
        grid=(b, n // tile),
        in_specs=in_specs,
        out_specs=[o[1] for o in outs],
        compiler_params=_cparams(("arbitrary", "arbitrary")),
        name="in_projection",
    )(*args)


def _rope_tables(n_tokens):
    t = jnp.arange(n_tokens, dtype=F32)
    row = jnp.floor(t / GRID_W)
    col = t - row * GRID_W
    inv_freq = ROPE_THETA ** (-jnp.arange(ROPE_FREQS, dtype=F32) / ROPE_FREQS)
    ang_row = row[:, None] * inv_freq[None, :]
    ang_col = col[:, None] * inv_freq[None, :]
    ang = jnp.concatenate([ang_row, ang_row, ang_col, ang_col] * (LANES // HEAD_DIM), axis=-1)
    first = (jnp.arange(LANES) % (2 * ROPE_FREQS)) < ROPE_FREQS
    sin = jnp.sin(ang)
    return jnp.cos(ang), jnp.where(first, -sin, 0.0), jnp.where(first, 0.0, sin)


def _stack_heads(q_rows, j, lo):
    qa = q_rows[:, (2 * j) * LANES:(2 * j + 1) * LANES]
    qb = q_rows[:, (2 * j + 1) * LANES:(2 * j + 2) * LANES]
    keep_lo = jnp.where(lo, 1.0, 0.0).astype(BF)
    keep_hi = jnp.where(lo, 0.0, 1.0).astype(BF)
    return jnp.concatenate([qa * keep_lo, qa * keep_hi, qb * keep_lo, qb * keep_hi], axis=0)


def _sink_column(sink_ref, j, t):
    return jnp.concatenate([jnp.full((t, 1), sink_ref[4 * j + a], F32) for a in range(4)], axis=0)


def _softmax_pv(scores, values, sink_col):
    m = sink_col
    for s in scores:
        m = jnp.maximum(m, jnp.max(s, axis=-1, keepdims=True))
    den = jnp.exp(sink_col - m)
    acc = None
    for s, v in zip(scores, values):
        p = jnp.exp(s - m)
        den = den + jnp.sum(p, axis=-1, keepdims=True)
        o = _dot(p.astype(BF), v)
        acc = o if acc is None else acc + o
    return acc / den


def _unstack_store(o_ref, o, rows, j, lo, t):
    o_ref[0, rows, (2 * j) * LANES:(2 * j + 1) * LANES] = jnp.where(lo, o[0:t], o[t:2 * t])
    o_ref[0, rows, (2 * j + 1) * LANES:(2 * j + 2) * LANES] = jnp.where(lo, o[2 * t:3 * t], o[3 * t:4 * t])


def _ctx_attn_kernel(sink_ref, q_ref, kk_ref, vv_ref, o_ref, *, n):
    lo = lax.broadcasted_iota(jnp.int32, (BLOCK, LANES), 1) < HEAD_DIM
    for sub in range(n // BLOCK):
        rows = slice(sub * BLOCK, (sub + 1) * BLOCK)
        q_rows = q_ref[0, rows, :]
        for j in range(KV_HEADS):
            cols = slice(j * LANES, (j + 1) * LANES)
            qs = _stack_heads(q_rows, j, lo)
            s = _dot_nt(qs, kk_ref[0, :, cols])
            o = _softmax_pv([s], [vv_ref[0, :, cols]], _sink_column(sink_ref, j, BLOCK))
            _unstack_store(o_ref, o, rows, j, lo, BLOCK)


def _context_attention(q, kk, vv, sink):
    b, n, _ = q.shape
    return pl.pallas_call(
        functools.partial(_ctx_attn_kernel, n=n),
        out_shape=jax.ShapeDtypeStruct((b, n, ATTN_WIDTH), F32),
        grid=(b,),
        in_specs=[pl.BlockSpec(memory_space=pltpu.SMEM),
                  pl.BlockSpec((1, n, ATTN_WIDTH), lambda i: (i, 0, 0)),
                  pl.BlockSpec((1, n, 2 * KV_WIDTH), lambda i: (i, 0, 0)),
                  pl.BlockSpec((1, n, 2 * KV_WIDTH), lambda i: (i, 0, 0))],
        out_specs=pl.BlockSpec((1, n, ATTN_WIDTH), lambda i: (i, 0, 0)),
        compiler_params=_cparams(("arbitrary",)),
        name="context_attention",
    )(sink, q, kk, vv)


def _lat_attn_kernel(sink_ref, q_ref, kkm_ref, kkp_ref, kkn_ref, vvm_ref, vvp_ref, vvn_ref,
                     ckk_ref, cvv_ref, o_ref, kext, vext, *, tq, n):
    t = pl.program_id(1)
    kext[0:BLOCK] = kkp_ref[0]
    kext[BLOCK:BLOCK + tq] = kkm_ref[0]
    kext[BLOCK + tq:2 * BLOCK + tq] = kkn_ref[0]
    vext[0:BLOCK] = vvp_ref[0]
    vext[BLOCK:BLOCK + tq] = vvm_ref[0]
    vext[BLOCK + tq:2 * BLOCK + tq] = vvn_ref[0]
    lo = lax.broadcasted_iota(jnp.int32, (BLOCK, LANES), 1) < HEAD_DIM
    r = lax.broadcasted_iota(jnp.int32, (BLOCK, 3 * BLOCK), 0)
    w = lax.broadcasted_iota(jnp.int32, (BLOCK, 3 * BLOCK), 1)
    band = (w >= r) & (w <= r + 2 * BLOCK)
    for sub in range(tq // BLOCK):
        rows = slice(sub * BLOCK, (sub + 1) * BLOCK)
        key = t * tq + (sub - 1) * BLOCK + w
        valid = jnp.where(band & (key >= 0) & (key < n), 1.0, 0.0)
        valid4 = jnp.concatenate([valid] * 4, axis=0) > 0.5
        q_rows = q_ref[0, rows, :]
        win = slice(sub * BLOCK, (sub + 3) * BLOCK)
        for j in range(KV_HEADS):
            cols = slice(j * LANES, (j + 1) * LANES)
            qs = _stack_heads(q_rows, j, lo)
            s_loc = jnp.where(valid4, _dot_nt(qs, kext[win, cols]), NEG_INF)
            s_ctx = _dot_nt(qs, ckk_ref[0, :, cols])
            o = _softmax_pv([s_loc, s_ctx], [vext[win, cols], cvv_ref[0, :, cols]],
                            _sink_column(sink_ref, j, BLOCK))
            _unstack_store(o_ref, o, rows, j, lo, BLOCK)


def _latent_attention(q, kk, vv, ckk, cvv, sink, tq):
    b, n, _ = q.shape
    past = ckk.shape[1]
    nblk = n // BLOCK
    per = tq // BLOCK
    kw = 2 * KV_WIDTH
    main = pl.BlockSpec((1, tq, kw), lambda i, t: (i, t, 0))
    prev = pl.BlockSpec((1, BLOCK, kw), lambda i, t: (i, jnp.maximum(t * per - 1, 0), 0))
    nxt = pl.BlockSpec((1, BLOCK, kw), lambda i, t: (i, jnp.minimum((t + 1) * per, nblk - 1), 0))
    ctx = pl.BlockSpec((1, past, kw), lambda i, t: (i, 0, 0))
    return pl.pallas_call(
        functools.partial(_lat_attn_kernel, tq=tq, n=n),
        out_shape=jax.ShapeDtypeStruct((b, n, ATTN_WIDTH), F32),
        grid=(b, n // tq),
        in_specs=[pl.BlockSpec(memory_space=pltpu.SMEM),
                  pl.BlockSpec((1, tq, ATTN_WIDTH), lambda i, t: (i, t, 0)),
                  main, prev, nxt, main, prev, nxt, ctx, ctx],
        out_specs=pl.BlockSpec((1, tq, ATTN_WIDTH), lambda i, t: (i, t, 0)),
        scratch_shapes=[pltpu.VMEM((tq + 2 * BLOCK, kw), BF), pltpu.VMEM((tq + 2 * BLOCK, kw), BF)],
        compiler_params=_cparams(("arbitrary", "arbitrary")),
        name="latent_attention",
    )(sink, q, kk, kk, kk, vv, vv, vv, ckk, cvv)


def _ssm_prepare(lam_re, lam_im, log_dt, b_re, b_im, c_re, c_im, d, w_glu, ntap):
    lam_re, lam_im = lam_re.astype(F32), lam_im.astype(F32)
    dt = jnp.exp(log_dt.astype(F32))[..., None]
    mag = jnp.exp(lam_re * dt)
    ang = lam_im * dt
    a_re, a_im = mag * jnp.cos(ang), mag * jnp.sin(ang)
    den = lam_re * lam_re + lam_im * lam_im
    num_re = a_re - 1.0
    coef_re = (num_re * lam_re + a_im * lam_im) / den
    coef_im = (a_im * lam_re - num_re * lam_im) / den
    br, bi = b_re.astype(F32), b_im.astype(F32)
    t_re = coef_re[..., None] * br - coef_im[..., None] * bi
    t_im = coef_re[..., None] * bi + coef_im[..., None] * br
    eye = jnp.eye(GROUPS_PER_BLOCK, dtype=F32)

    def block_diag_in(t):
        t = t.reshape(2, N_GROUP_BLOCKS, GROUPS_PER_BLOCK, SSM_STATE, SSM_CH)
        return jnp.einsum("dbgsp,gh->dbgphs", t, eye).reshape(2, N_GROUP_BLOCKS, LANES, STATE_LANES)

    taps = []
    p_re, p_im = jnp.ones_like(a_re), jnp.zeros_like(a_re)
    for _ in range(ntap):
        k_re = p_re[..., None] * t_re - p_im[..., None] * t_im
        k_im = p_re[..., None] * t_im + p_im[..., None] * t_re
        taps.append(jnp.concatenate([block_diag_in(k_re), block_diag_in(k_im)], axis=-1))
        p_re, p_im = p_re * a_re - p_im * a_im, p_re * a_im + p_im * a_re
    w_b = jnp.concatenate(taps, axis=2).astype(BF)

    def block_diag_out(c):
        c = c.astype(F32).reshape(2, N_GROUP_BLOCKS, GROUPS_PER_BLOCK, SSM_CH, SSM_STATE)
        return jnp.einsum("dbgps,gh->dbgshp", c, eye).reshape(2, N_GROUP_BLOCKS, STATE_LANES, LANES)

    w_c = jnp.concatenate([block_diag_out(c_re), -block_diag_out(c_im)], axis=2).astype(BF)

    def lanes(z):
        return z.reshape(2, N_GROUP_BLOCKS, 1, STATE_LANES)

    coef = jnp.concatenate([lanes(p_re), lanes(p_im), lanes(a_re), lanes(a_im)], axis=2)
    wg = w_glu.astype(F32).reshape(N_GROUP_BLOCKS, GROUPS_PER_BLOCK, SSM_CH, SSM_CH)
    w_g = jnp.einsum("bgpq,gh->bgphq", wg, eye).reshape(N_GROUP_BLOCKS, LANES, LANES).astype(BF)
    return w_b, w_c, coef, d.astype(F32).reshape(1, SSM_WIDTH), w_g


def _ssm_kernel(u_ref, wb_ref, wc_ref, coef_ref, h0re_ref, h0im_ref, d_ref, wg_ref,
                y_ref, fre_ref, fim_ref, ybwd, ubuf, vstate, *bufs, total, chunk, step_rows, bsz, ntap):
    bu = (bufs[0:2], bufs[2:4])
    hbf = (bufs[4:6], bufs[6:8])
    n_chunks = total // chunk
    n_steps = chunk // step_rows
    pack_steps = 2 * SUBLANES // step_rows if step_rows < 2 * SUBLANES else 1
    row = lax.broadcasted_iota(jnp.int32, (step_rows, STATE_LANES), 0)

    def project(d, c, slot):
        r0 = pl.multiple_of(c * chunk, chunk)
        u = u_ref[pl.ds(r0, chunk), :]
        if ntap > 1:
            if d == 0:
                ubuf[d, SUBLANES:SUBLANES + chunk] = u
            else:
                ubuf[d, 0:chunk] = u
            shifted = ubuf[d, bsz:bsz + chunk]
            if d == 0:
                ubuf[d, 0:SUBLANES] = u[chunk - SUBLANES:chunk]
            else:
                ubuf[d, chunk:chunk + SUBLANES] = u[0:SUBLANES]
            lhs = jnp.concatenate([u, shifted], axis=1)
        else:
            lhs = u
        bu[slot][d][...] = _dot(lhs.astype(BF), wb_ref[d, 0])

    def scan(slot, first):
        ar, ai, v_r, v_i, e_re, e_im = [], [], [], [], [], []
        for d in range(2):
            coef = coef_ref[d, 0]
            ar.append(jnp.broadcast_to(coef[0:1], (step_rows, STATE_LANES)))
            ai.append(jnp.broadcast_to(coef[1:2], (step_rows, STATE_LANES)))
            v_r.append(vstate[d, 0])
            v_i.append(vstate[d, 1])
            if ntap > 1:
                h0r, h0i = h0re_ref[d, 0], h0im_ref[d, 0]
                a_re, a_im = coef[2:3], coef[3:4]
                older = (row < bsz) if d == 0 else (row >= bsz)
                e_re.append(jnp.where(older, a_re * h0r - a_im * h0i, 0.0) * first)
                e_im.append(jnp.where(older, a_re * h0i + a_im * h0r, 0.0) * first)
        held = [[], []]
        for k in range(n_steps):
            for d in range(2):
                s = k if d == 0 else n_steps - 1 - k
                rows = slice(s * step_rows, (s + 1) * step_rows)
                b_r = bu[slot][d][rows, 0:STATE_LANES]
                b_i = bu[slot][d][rows, STATE_LANES:2 * STATE_LANES]
                if ntap > 1 and k == 0:
                    b_r, b_i = b_r + e_re[d], b_i + e_im[d]
                v_r[d], v_i[d] = ar[d] * v_r[d] - ai[d] * v_i[d] + b_r, ar[d] * v_i[d] + ai[d] * v_r[d] + b_i
                held[d].append((s, v_r[d], v_i[d]))
                if len(held[d]) == pack_steps:
                    hs = sorted(held[d], key=lambda h: h[0])
                    r16 = slice(hs[0][0] * step_rows, (hs[-1][0] + 1) * step_rows)
                    hbf[slot][d][r16, 0:STATE_LANES] = jnp.concatenate([h[1] for h in hs], axis=0).astype(BF)
                    hbf[slot][d][r16, STATE_LANES:2 * STATE_LANES] = jnp.concatenate(
                        [h[2] for h in hs], axis=0).astype(BF)
                    held[d] = []
        for d in range(2):
            vstate[d, 0] = v_r[d]
            vstate[d, 1] = v_i[d]

    def readout(slot, c):
        for d in range(2):
            y_part = _dot(hbf[slot][d][...], wc_ref[d, 0])
            cc = c if d == 0 else n_chunks - 1 - c
            r0 = pl.multiple_of(cc * chunk, chunk)
            if d == 0:
                y_ref[pl.ds(r0, chunk), :] = y_part
            else:
                ybwd[pl.ds(r0, chunk), :] = y_part

    for d in range(2):
        h0r, h0i = h0re_ref[d, 0], h0im_ref[d, 0]
        if ntap > 1:
            newer = (row >= bsz) if d == 0 else (row < bsz)
            vstate[d, 0] = jnp.where(newer, h0r, 0.0)
            vstate[d, 1] = jnp.where(newer, h0i, 0.0)
        else:
            vstate[d, 0] = h0r
            vstate[d, 1] = h0i
    if ntap > 1:
        ubuf[...] = jnp.zeros_like(ubuf)
    for d in range(2):
        hbf[1][d][...] = jnp.zeros((chunk, 2 * STATE_LANES), BF)
    project(0, 0, 0)
    project(1, n_chunks - 1, 0)

    def body(pair, carry):
        for slot in range(2):
            c = 2 * pair + slot
            first = jnp.where(c == 0, 1.0, 0.0)
            project(0, jnp.minimum(c + 1, n_chunks - 1), 1 - slot)
            project(1, jnp.maximum(n_chunks - 2 - c, 0), 1 - slot)
            readout(1 - slot, jnp.maximum(c - 1, 0))
            scan(slot, first)
        return carry

    lax.fori_loop(0, n_chunks // 2, body, 0)
    readout(1, n_chunks - 1)
    for d in range(2):
        fre_ref[d, 0] = vstate[d, 0]
        fim_ref[d, 0] = vstate[d, 1]

    fin_rows = 4 * chunk

    def finish(c, carry):
        r0 = pl.multiple_of(c * fin_rows, fin_rows)
        rows = pl.ds(r0, fin_rows)
        y = y_ref[rows, :] + ybwd[rows, :] + u_ref[rows, :] * d_ref[...]
        x = jax.nn.gelu(y, approximate=True)
        y_ref[rows, :] = x * jax.nn.sigmoid(_dot(x.astype(BF), wg_ref[0]))
        return carry

    lax.fori_loop(0, total // fin_rows, finish, 0)


def _s5_scan(u_tm, prep, h0_re, h0_im, bsz, ntap, rows):
    w_b, w_c, coef, d_vec, w_g = prep
    total = u_tm.shape[0]
    chunk = rows
    step_rows = bsz * ntap
    kb = LANES * ntap

    dir_spec = lambda shape: pl.BlockSpec((2, 1) + shape, lambda g: (0, g, 0, 0))
    state_shape = jax.ShapeDtypeStruct((2, N_GROUP_BLOCKS, step_rows, STATE_LANES), F32)
    return pl.pallas_call(
        functools.partial(_ssm_kernel, total=total, chunk=chunk, step_rows=step_rows, bsz=bsz, ntap=ntap),
        out_shape=[jax.ShapeDtypeStruct((total, SSM_WIDTH), F32), state_shape, state_shape],
        grid=(N_GROUP_BLOCKS,),
        in_specs=[pl.BlockSpec((total, LANES), lambda g: (0, g)),
                  dir_spec((kb, 2 * STATE_LANES)),
                  dir_spec((2 * STATE_LANES, LANES)),
                  dir_spec((4, STATE_LANES)),
                  dir_spec((step_rows, STATE_LANES)),
                  dir_spec((step_rows, STATE_LANES)),
                  pl.BlockSpec((1, LANES), lambda g: (0, g)),
                  pl.BlockSpec((1, LANES, LANES), lambda g: (g, 0, 0))],
        out_specs=[pl.BlockSpec((total, LANES), lambda g: (0, g)),
                   dir_spec((step_rows, STATE_LANES)),
                   dir_spec((step_rows, STATE_LANES))],
        scratch_shapes=[pltpu.VMEM((total, LANES), F32),
                        pltpu.VMEM((2, chunk + SUBLANES, LANES), F32),
                        pltpu.VMEM((2, 2, step_rows, STATE_LANES), F32)]
        + [pltpu.VMEM((chunk, 2 * STATE_LANES), F32)] * 4
        + [pltpu.VMEM((chunk, 2 * STATE_LANES), BF)] * 4,
        compiler_params=_cparams(("arbitrary",)),
        name="s5_scan",
    )(u_tm, w_b, w_c, coef, h0_re, h0_im, d_vec, w_g)


def _merge_kernel(attn_ref, ssm_ref, x_ref, gt_ref, ga_ref, gs_ref, w_ref, o_ref):
    na = _rms(attn_ref[0]) * ga_ref[...]
    ns = _rms(ssm_ref[0]) * gs_ref[...]
    mo = _dot(na.astype(BF), w_ref[0:ATTN_WIDTH, :]) + _dot(ns.astype(BF), w_ref[ATTN_WIDTH:D_MODEL, :])
    o_ref[0] = x_ref[0] + gt_ref[0] * mo


def _merge(attn, ssm, x, gate, g_attn, g_ssm, w_out, tile):
    b, n, _ = x.shape
    per_batch = gate.shape[0] > 1
    row = lambda width: pl.BlockSpec((1, tile, width), lambda i, t: (i, t, 0))
    return pl.pallas_call(
        _merge_kernel,
        out_shape=jax.ShapeDtypeStruct((b, n, D_MODEL), F32),
        grid=(b, n // tile),
        in_specs=[row(ATTN_WIDTH), row(SSM_WIDTH), row(D_MODEL),
                  pl.BlockSpec((1, 1, D_MODEL), (lambda i, t: (i, 0, 0)) if per_batch else (lambda i, t: (0, 0, 0))),
                  _const_spec((1, ATTN_WIDTH)), _const_spec((1, SSM_WIDTH)),
                  _const_spec((D_MODEL, D_MODEL))],
        out_specs=row(D_MODEL),
        compiler_params=_cparams(("arbitrary", "arbitrary")),
        name="merge_projection",
    )(attn, ssm, x, gate, g_attn.reshape(1, -1), g_ssm.reshape(1, -1), w_out)


HALO = 16


def _ffn_kernel(x_ref, xp_ref, xn_ref, sh_ref, sc_ref, gt_ref, g2_ref, wup_ref, cw_ref, cb_ref, wdn_ref, gf_ref,
                o_ref, hext, upbuf, act, *, tile, n_tiles):
    t = pl.program_id(1)

    def modulated(x):
        h = _rms(x) * g2_ref[...]
        return (h * (1.0 + sc_ref[0]) + sh_ref[0]).astype(BF)

    x = x_ref[0]
    hext[0:HALO] = jnp.where(t > 0, modulated(xp_ref[0]), jnp.zeros((HALO, D_MODEL), BF))
    hext[HALO:HALO + tile] = modulated(x)
    hext[HALO + tile:2 * HALO + tile] = jnp.where(t < n_tiles - 1, modulated(xn_ref[0]),
                                                  jnp.zeros((HALO, D_MODEL), BF))
    h_all = hext[...]
    for c in range(D_FF // FF_CHUNK):
        halves = []
        for base in (c * FF_CHUNK, D_FF + c * FF_CHUNK):
            cols = slice(base, base + FF_CHUNK)
            upbuf[...] = _dot(h_all, wup_ref[:, cols])
            conv = (upbuf[HALO - 1:HALO - 1 + tile] * cw_ref[0:1, cols]
                    + upbuf[HALO:HALO + tile] * cw_ref[1:2, cols]
                    + upbuf[HALO + 1:HALO + 1 + tile] * cw_ref[2:3, cols]
                    + cb_ref[:, cols])
            halves.append(conv)
        gate, val = halves
        act[:, c * FF_CHUNK:(c + 1) * FF_CHUNK] = (gate * jax.nn.sigmoid(gate) * val).astype(BF)
    y = x + gt_ref[0] * _dot(act[...], wdn_ref[...])
    o_ref[0] = _rms(y) * gf_ref[...]


def _conv_ffn(x, shift, scale, gate, g2, w_up, conv_w, conv_b, w_down, g_final, tile):
    b, n, _ = x.shape
    n_tiles = n // tile
    per = tile // HALO
    nh = n // HALO
    per_batch = shift.shape[0] > 1
    mod_spec = pl.BlockSpec((1, 1, D_MODEL), (lambda i, t: (i, 0, 0)) if per_batch else (lambda i, t: (0, 0, 0)))
    return pl.pallas_call(
        functools.partial(_ffn_kernel, tile=tile, n_tiles=n_tiles),
        out_shape=jax.ShapeDtypeStruct((b, n, D_MODEL), F32),
        grid=(b, n_tiles),
        in_specs=[pl.BlockSpec((1, tile, D_MODEL), lambda i, t: (i, t, 0)),
                  pl.BlockSpec((1, HALO, D_MODEL), lambda i, t: (i, jnp.maximum(t * per - 1, 0), 0)),
                  pl.BlockSpec((1, HALO, D_MODEL), lambda i, t: (i, jnp.minimum((t + 1) * per, nh - 1), 0)),
                  mod_spec, mod_spec, mod_spec,
                  _const_spec((1, D_MODEL)),
                  _const_spec((D_MODEL, 2 * D_FF)),
                  _const_spec((3, 2 * D_FF)),
                  _const_spec((1, 2 * D_FF)),
                  _const_spec((D_FF, D_MODEL)),
                  _const_spec((1, D_MODEL))],
        out_specs=pl.BlockSpec((1, tile, D_MODEL), lambda i, t: (i, t, 0)),
        scratch_shapes=[pltpu.VMEM((tile + 2 * HALO, D_MODEL), BF),
                        pltpu.VMEM((tile + 2 * HALO, FF_CHUNK), F32),
                        pltpu.VMEM((tile, D_FF), BF)],
        compiler_params=_cparams(("arbitrary", "arbitrary")),
        name="conv_ffn",
    )(x, x, x, shift, scale, gate, g2.reshape(1, -1), w_up, conv_w, conv_b.reshape(1, -1), w_down,
      g_final.reshape(1, -1))


def _dup_heads(w):
    a, b = w[..., :HEAD_DIM], w[..., HEAD_DIM:]
    return jnp.concatenate([a, a, b, b], axis=-1)


def _stream(x, mods, weights, rope_tabs, ctx_kv, h0, ssm_prep, with_kv, tile, ssm_rows, attn_tile):
    sh1, sc1, gt1, sh2, sc2, gt2 = mods
    (g1, g2, w_in, sink, g_attn, g_ssm, w_out, w_up, conv_w, conv_b, w_down, g_final) = weights
    b, n, _ = x.shape
    outs = _in_projection(x, sh1, sc1, g1, w_in, rope_tabs, with_kv, tile)
    q, kk, vv, u = outs[:4]
    if ctx_kv is None:
        attn = _context_attention(q, kk, vv, sink)
    else:
        attn = _latent_attention(q, kk, vv, ctx_kv[0], ctx_kv[1], sink, attn_tile)
    ntap = max(1, SUBLANES // b)
    u_tm = u.transpose(1, 0, 2).reshape(n * b, SSM_WIDTH)
    y_tm, fin_re, fin_im = _s5_scan(u_tm, ssm_prep[ntap], h0[0], h0[1], b, ntap, ssm_rows)
    ssm = y_tm.reshape(n, b, SSM_WIDTH).transpose(1, 0, 2)
    x1 = _merge(attn, ssm, x, gt1, g_attn, g_ssm, w_out, tile)
    y = _conv_ffn(x1, sh2, sc2, gt2, g2, w_up, conv_w, conv_b, w_down, g_final, tile)
    return y, outs[4:], (fin_re, fin_im)


def kernel(x_prompt, x_sample, cache_k, cache_v, state_ssm_re, state_ssm_im, c, c_ctx, w_ada, b_ada, g_norm1, g_norm2, w_in, attn_sink, ssm_lam_re, ssm_lam_im, ssm_log_dt, ssm_b_re, ssm_b_im, ssm_c_re, ssm_c_im, ssm_d, ssm_w_glu, g_out_attn, g_out_ssm, w_out, w_up, conv_w, conv_b, w_down, g_final):
    depth = w_in.shape[0]
    assert depth == 1
    l = 0
    bp, seq, _ = x_prompt.shape
    bs, dec_seq, _ = x_sample.shape
    past = cache_k.shape[2]

    cond = jnp.concatenate([c_ctx[None, :], c, jnp.zeros((SUBLANES - 1 - bs, D_MODEL), F32)], axis=0)
    mod = _adaln(cond, w_ada[l], b_ada[l])
    mods_ctx = [mod[0:1, i * D_MODEL:(i + 1) * D_MODEL].reshape(1, 1, D_MODEL) for i in range(6)]
    mods_lat = [mod[1:1 + bs, i * D_MODEL:(i + 1) * D_MODEL].reshape(bs, 1, D_MODEL) for i in range(6)]

    wi = w_in[l]
    wq = wi[:, :ATTN_WIDTH]
    wk = wi[:, ATTN_WIDTH:ATTN_WIDTH + KV_WIDTH]
    wv = wi[:, ATTN_WIDTH + KV_WIDTH:ATTN_WIDTH + 2 * KV_WIDTH]
    wu = wi[:, ATTN_WIDTH + 2 * KV_WIDTH:]
    w_in_lat = jnp.concatenate([wq, _dup_heads(wk), _dup_heads(wv), wu], axis=1).astype(BF)
    w_in_ctx = jnp.concatenate([w_in_lat, wk.astype(BF), wv.astype(BF)], axis=1)
    weights = (g_norm1[l], g_norm2[l], None, attn_sink[l], g_out_attn[l], g_out_ssm[l], w_out[l].astype(BF),
               w_up[l].astype(BF), conv_w[l], conv_b[l], w_down[l].astype(BF), g_final)

    ssm_params = (ssm_lam_re[l], ssm_lam_im[l], ssm_log_dt[l], ssm_b_re[l], ssm_b_im[l],
                  ssm_c_re[l], ssm_c_im[l], ssm_d[l], ssm_w_glu[l])
    taps = {max(1, SUBLANES // bp), max(1, SUBLANES // bs)}
    ssm_prep = {nt: _ssm_prepare(*ssm_params, nt) for nt in taps}

    zeros_h0 = jnp.zeros((2, N_GROUP_BLOCKS, bp, STATE_LANES), F32)
    w_ctx = weights[:2] + (w_in_ctx,) + weights[3:]
    y_prompt, (k_new, v_new), (fin_re, fin_im) = _stream(
        x_prompt, mods_ctx, w_ctx, None, None, (zeros_h0, zeros_h0), ssm_prep, True,
        tile=seq, ssm_rows=256, attn_tile=None)
    new_k = k_new.reshape(bp, 1, seq, KV_HEADS, HEAD_DIM)
    new_v = v_new.reshape(bp, 1, seq, KV_HEADS, HEAD_DIM)

    def states_out(fin):
        return fin.transpose(2, 0, 1, 3).reshape(bp, 1, 2, SSM_GROUPS, SSM_STATE)

    ckk = _dup_heads(cache_k[:, l].reshape(bs, past, KV_WIDTH)).astype(BF)
    cvv = _dup_heads(cache_v[:, l].reshape(bs, past, KV_WIDTH)).astype(BF)

    def h0_in(state):
        s = state.astype(F32).reshape(bs, 2, N_GROUP_BLOCKS, STATE_LANES).transpose(1, 2, 0, 3)
        return jnp.concatenate([s] * max(1, SUBLANES // bs), axis=2)

    w_lat = weights[:2] + (w_in_lat,) + weights[3:]
    y_sample, _, _ = _stream(
        x_sample, mods_lat, w_lat, _rope_tables(dec_seq), (ckk, cvv),
        (h0_in(state_ssm_re[:, l]), h0_in(state_ssm_im[:, l])), ssm_prep, False,
        tile=512, ssm_rows=256, attn_tile=512)

    return (y_prompt, y_sample, new_k, new_v, states_out(fin_re), states_out(fin_im))
```

```python
import functools
import math

import jax
import jax.numpy as jnp
from jax import lax
from jax.experimental import pallas as pl
from jax.experimental.pallas import tpu as pltpu

D_MODEL = 1024
HEAD_DIM = 64
N_HEADS = 8
KV_HEADS = 2
ATTN_WIDTH = N_HEADS * HEAD_DIM
KV_WIDTH = KV_HEADS * HEAD_DIM
BLOCK = 128
GRID_W = 64
ROPE_THETA = 10000.0
ROPE_FREQS = HEAD_DIM // 4
SSM_CH = 16
SSM_WIDTH = D_MODEL - ATTN_WIDTH
SSM_GROUPS = SSM_WIDTH // SSM_CH
SSM_STATE = 64
D_FF = 2816
EPS = 1e-6
NEG_INF = -1e30

LANES = 128
SUBLANES = 8
GROUPS_PER_BLOCK = LANES // SSM_CH
N_GROUP_BLOCKS = SSM_GROUPS // GROUPS_PER_BLOCK
STATE_LANES = GROUPS_PER_BLOCK * SSM_STATE
FF_CHUNK = 256
VMEM_LIMIT = 56 * 1024 * 1024

BF = jnp.bfloat16
F32 = jnp.float32


def _cparams(sem):
    return pltpu.CompilerParams(dimension_semantics=sem, vmem_limit_bytes=VMEM_LIMIT)


def _const_spec(shape):
    nd = len(shape)
    return pl.BlockSpec(shape, lambda *_: (0,) * nd, pipeline_mode=pl.Buffered(1))


def _rms(x):
    return x * lax.rsqrt(jnp.mean(x * x, axis=-1, keepdims=True) + EPS)


def _dot(a, b):
    return jnp.dot(a, b, preferred_element_type=F32)


def _dot_nt(a, b):
    return lax.dot_general(a, b, (((1,), (1,)), ((), ())), preferred_element_type=F32)


def _ada_kernel(cond_ref, w_ref, b_ref, o_ref):
    c = cond_ref[...]
    s = c * jax.nn.sigmoid(c)
    o_ref[...] = _dot(s.astype(BF), w_ref[...].astype(BF)) + b_ref[...]


def _adaln(cond, w_ada, b_ada):
    n = cond.shape[0]
    return pl.pallas_call(
        _ada_kernel,
        out_shape=jax.ShapeDtypeStruct((n, 6 * D_MODEL), F32),
        grid=(6,),
        in_specs=[pl.BlockSpec((n, D_MODEL), lambda i: (0, 0)),
                  pl.BlockSpec((D_MODEL, D_MODEL), lambda i: (0, i)),
                  pl.BlockSpec((1, D_MODEL), lambda i: (0, i))],
        out_specs=pl.BlockSpec((n, D_MODEL), lambda i: (0, i)),
        compiler_params=_cparams(("arbitrary",)),
        name="adaln",
    )(cond, w_ada, b_ada.reshape(1, -1))


def _rope(z, cos, sin_lo, sin_hi):
    return z * cos + pltpu.roll(z, LANES - ROPE_FREQS, 1) * sin_lo + pltpu.roll(z, ROPE_FREQS, 1) * sin_hi


def _inproj_kernel(*refs, rope, with_kv, tile, bsz):
    x_ref, sh_ref, sc_ref, g_ref, w_ref = refs[:5]
    pos = 5
    if rope:
        cos_ref, slo_ref, shi_ref = refs[pos:pos + 3]
        pos += 3
    q_ref, kk_ref, vv_ref, u_ref = refs[pos:pos + 4]
    pos += 4
    h = _rms(x_ref[0]) * g_ref[...]
    h = h * (1.0 + sc_ref[0]) + sh_ref[0]
    proj = _dot(h.astype(BF), w_ref[...])
    q = proj[:, :ATTN_WIDTH] * (HEAD_DIM ** -0.5)
    kk = proj[:, ATTN_WIDTH:ATTN_WIDTH + 2 * KV_WIDTH]
    vv = proj[:, ATTN_WIDTH + 2 * KV_WIDTH:ATTN_WIDTH + 4 * KV_WIDTH]
    u0 = ATTN_WIDTH + 4 * KV_WIDTH
    bi = pl.program_id(1)
    for k in range(SSM_WIDTH // LANES):
        u_ref[k, pl.ds(bi, tile, stride=bsz), :] = proj[:, u0 + k * LANES:u0 + (k + 1) * LANES]
    vv_ref[0] = vv.astype(BF)
    if rope:
        cos, slo, shi = cos_ref[...], slo_ref[...], shi_ref[...]
        for c in range(ATTN_WIDTH // LANES):
            q_ref[0, :, c * LANES:(c + 1) * LANES] = _rope(
                q[:, c * LANES:(c + 1) * LANES], cos, slo, shi).astype(BF)
        for c in range(2 * KV_WIDTH // LANES):
            kk_ref[0, :, c * LANES:(c + 1) * LANES] = _rope(
                kk[:, c * LANES:(c + 1) * LANES], cos, slo, shi).astype(BF)
    else:
        q_ref[0] = q.astype(BF)
        kk_ref[0] = kk.astype(BF)
    if with_kv:
        k_ref, v_ref = refs[pos:pos + 2]
        k0 = u0 + SSM_WIDTH
        k_ref[0] = proj[:, k0:k0 + KV_WIDTH]
        v_ref[0] = proj[:, k0 + KV_WIDTH:k0 + 2 * KV_WIDTH]


def _in_projection(x, shift, scale, g, w, rope_tabs, with_kv, tile):
    b, n, _ = x.shape
    nc = w.shape[1]
    per_batch = shift.shape[0] > 1
    mod_spec = pl.BlockSpec((1, 1, D_MODEL), (lambda t, i: (i, 0, 0)) if per_batch else (lambda t, i: (0, 0, 0)))
    in_specs = [pl.BlockSpec((1, tile, D_MODEL), lambda t, i: (i, t, 0)),
                mod_spec, mod_spec,
                _const_spec((1, D_MODEL)),
                _const_spec((D_MODEL, nc))]
    args = [x, shift, scale, g.reshape(1, -1), w]
    if rope_tabs is not None:
        in_specs += [pl.BlockSpec((tile, LANES), lambda t, i: (t, 0))] * 3
        args += list(rope_tabs)

    def out(width, dtype):
        return (jax.ShapeDtypeStruct((b, n, width), dtype),
                pl.BlockSpec((1, tile, width), lambda t, i: (i, t, 0)))

    n_slabs = SSM_WIDTH // LANES
    u_out = (jax.ShapeDtypeStruct((n_slabs, n * b, LANES), F32),
             pl.BlockSpec((n_slabs, tile * b, LANES), lambda t, i: (0, t, 0)))
    outs = [out(ATTN_WIDTH, BF), out(2 * KV_WIDTH, BF), out(2 * KV_WIDTH, BF), u_out]
    if with_kv:
        outs += [out(KV_WIDTH, F32), out(KV_WIDTH, F32)]
    return pl.pallas_call(
        functools.partial(_inproj_kernel, rope=rope_tabs is not None, with_kv=with_kv, tile=tile, bsz=b),
        out_shape=[o[0] for o in outs],
        grid=(n // tile, b),
        in_specs=in_specs,
        out_specs=[o[1] for o in outs],
        compiler_params=_cparams(("arbitrary", "arbitrary")),
        name="in_projection",
    )(*args)


def _rope_tables(n_tokens):
    t = jnp.arange(n_tokens, dtype=F32)
    row = jnp.floor(t / GRID_W)
    col = t - row * GRID_W
    inv_freq = ROPE_THETA ** (-jnp.arange(ROPE_FREQS, dtype=F32) / ROPE_FREQS)
    ang_row = row[:, None] * inv_freq[None, :]
    ang_col = col[:, None] * inv_freq[None, :]
    ang = jnp.concatenate([ang_row, ang_row, ang_col, ang_col] * (LANES // HEAD_DIM), axis=-1)
    first = (jnp.arange(LANES) % (2 * ROPE_FREQS)) < ROPE_FREQS
    sin = jnp.sin(ang)
    return jnp.cos(ang), jnp.where(first, -sin, 0.0), jnp.where(first, 0.0, sin)


def _stack_heads(q_rows, j, lo):
    qa = q_rows[:, (2 * j) * LANES:(2 * j + 1) * LANES]
    qb = q_rows[:, (2 * j + 1) * LANES:(2 * j + 2) * LANES]
    keep_lo = jnp.where(lo, 1.0, 0.0).astype(BF)
    keep_hi = jnp.where(lo, 0.0, 1.0).astype(BF)
    return jnp.concatenate([qa * keep_lo, qa * keep_hi, qb * keep_lo, qb * keep_hi], axis=0)


def _sink_column(sink_ref, j, t):
    return jnp.concatenate([jnp.full((t, 1), sink_ref[4 * j + a], F32) for a in range(4)], axis=0)


def _softmax_pv(scores, values, sink_col):
    m = sink_col
    for s in scores:
        m = jnp.maximum(m, jnp.max(s, axis=-1, keepdims=True))
    den = jnp.exp(sink_col - m)
    acc = None
    for s, v in zip(scores, values):
        p = jnp.exp(s - m)
        den = den + jnp.sum(p, axis=-1, keepdims=True)
        o = _dot(p.astype(BF), v)
        acc = o if acc is None else acc + o
    return acc / den


def _unstack_store(o_ref, o, rows, j, lo, t):
    o_ref[0, rows, (2 * j) * LANES:(2 * j + 1) * LANES] = jnp.where(lo, o[0:t], o[t:2 * t])
    o_ref[0, rows, (2 * j + 1) * LANES:(2 * j + 2) * LANES] = jnp.where(lo, o[2 * t:3 * t], o[3 * t:4 * t])


def _ctx_attn_kernel(sink_ref, q_ref, kk_ref, vv_ref, o_ref, *, n):
    lo = lax.broadcasted_iota(jnp.int32, (BLOCK, LANES), 1) < HEAD_DIM
    for sub in range(n // BLOCK):
        rows = slice(sub * BLOCK, (sub + 1) * BLOCK)
        q_rows = q_ref[0, rows, :]
        for j in range(KV_HEADS):
            cols = slice(j * LANES, (j + 1) * LANES)
            qs = _stack_heads(q_rows, j, lo)
            s = _dot_nt(qs, kk_ref[0, :, cols])
            o = _softmax_pv([s], [vv_ref[0, :, cols]], _sink_column(sink_ref, j, BLOCK))
            _unstack_store(o_ref, o, rows, j, lo, BLOCK)


def _context_attention(q, kk, vv, sink):
    b, n, _ = q.shape
    return pl.pallas_call(
        functools.partial(_ctx_attn_kernel, n=n),
        out_shape=jax.ShapeDtypeStruct((b, n, ATTN_WIDTH), F32),
        grid=(b,),
        in_specs=[pl.BlockSpec(memory_space=pltpu.SMEM),
                  pl.BlockSpec((1, n, ATTN_WIDTH), lambda i: (i, 0, 0)),
                  pl.BlockSpec((1, n, 2 * KV_WIDTH), lambda i: (i, 0, 0)),
                  pl.BlockSpec((1, n, 2 * KV_WIDTH), lambda i: (i, 0, 0))],
        out_specs=pl.BlockSpec((1, n, ATTN_WIDTH), lambda i: (i, 0, 0)),
        compiler_params=_cparams(("arbitrary",)),
        name="context_attention",
    )(sink, q, kk, vv)


def _lat_attn_kernel(sink_ref, q_ref, kkm_ref, kkp_ref, kkn_ref, vvm_ref, vvp_ref, vvn_ref,
                     ckk_ref, cvv_ref, o_ref, kext, vext, *, tq, n):
    t = pl.program_id(1)
    kext[0:BLOCK] = kkp_ref[0]
    kext[BLOCK:BLOCK + tq] = kkm_ref[0]
    kext[BLOCK + tq:2 * BLOCK + tq] = kkn_ref[0]
    vext[0:BLOCK] = vvp_ref[0]
    vext[BLOCK:BLOCK + tq] = vvm_ref[0]
    vext[BLOCK + tq:2 * BLOCK + tq] = vvn_ref[0]
    lo = lax.broadcasted_iota(jnp.int32, (BLOCK, LANES), 1) < HEAD_DIM
    r = lax.broadcasted_iota(jnp.int32, (BLOCK, 3 * BLOCK), 0)
    w = lax.broadcasted_iota(jnp.int32, (BLOCK, 3 * BLOCK), 1)
    band = (w >= r) & (w <= r + 2 * BLOCK)
    for sub in range(tq // BLOCK):
        rows = slice(sub * BLOCK, (sub + 1) * BLOCK)
        key = t * tq + (sub - 1) * BLOCK + w
        valid = jnp.where(band & (key >= 0) & (key < n), 1.0, 0.0)
        valid4 = jnp.concatenate([valid] * 4, axis=0) > 0.5
        q_rows = q_ref[0, rows, :]
        win = slice(sub * BLOCK, (sub + 3) * BLOCK)
        for j in range(KV_HEADS):
            cols = slice(j * LANES, (j + 1) * LANES)
            qs = _stack_heads(q_rows, j, lo)
            s_loc = jnp.where(valid4, _dot_nt(qs, kext[win, cols]), NEG_INF)
            s_ctx = _dot_nt(qs, ckk_ref[0, :, cols])
            o = _softmax_pv([s_loc, s_ctx], [vext[win, cols], cvv_ref[0, :, cols]],
                            _sink_column(sink_ref, j, BLOCK))
            _unstack_store(o_ref, o, rows, j, lo, BLOCK)


def _latent_attention(q, kk, vv, ckk, cvv, sink, tq):
    b, n, _ = q.shape
    past = ckk.shape[1]
    nblk = n // BLOCK
    per = tq // BLOCK
    kw = 2 * KV_WIDTH
    main = pl.BlockSpec((1, tq, kw), lambda i, t: (i, t, 0))
    prev = pl.BlockSpec((1, BLOCK, kw), lambda i, t: (i, jnp.maximum(t * per - 1, 0), 0))
    nxt = pl.BlockSpec((1, BLOCK, kw), lambda i, t: (i, jnp.minimum((t + 1) * per, nblk - 1), 0))
    ctx = pl.BlockSpec((1, past, kw), lambda i, t: (i, 0, 0))
    return pl.pallas_call(
        functools.partial(_lat_attn_kernel, tq=tq, n=n),
        out_shape=jax.ShapeDtypeStruct((b, n, ATTN_WIDTH), F32),
        grid=(b, n // tq),
        in_specs=[pl.BlockSpec(memory_space=pltpu.SMEM),
                  pl.BlockSpec((1, tq, ATTN_WIDTH), lambda i, t: (i, t, 0)),
                  main, prev, nxt, main, prev, nxt, ctx, ctx],
        out_specs=pl.BlockSpec((1, tq, ATTN_WIDTH), lambda i, t: (i, t, 0)),
        scratch_shapes=[pltpu.VMEM((tq + 2 * BLOCK, kw), BF), pltpu.VMEM((tq + 2 * BLOCK, kw), BF)],
        compiler_params=_cparams(("arbitrary", "arbitrary")),
        name="latent_attention",
    )(sink, q, kk, kk, kk, vv, vv, vv, ckk, cvv)


def _ssm_prepare(lam_re, lam_im, log_dt, b_re, b_im, c_re, c_im, d, w_glu, ntap):
    lam_re, lam_im = lam_re.astype(F32), lam_im.astype(F32)
    dt = jnp.exp(log_dt.astype(F32))[..., None]
    mag = jnp.exp(lam_re * dt)
    ang = lam_im * dt
    a_re, a_im = mag * jnp.cos(ang), mag * jnp.sin(ang)
    den = lam_re * lam_re + lam_im * lam_im
    num_re = a_re - 1.0
    coef_re = (num_re * lam_re + a_im * lam_im) / den
    coef_im = (a_im * lam_re - num_re * lam_im) / den
    br, bi = b_re.astype(F32), b_im.astype(F32)
    t_re = coef_re[..., None] * br - coef_im[..., None] * bi
    t_im = coef_re[..., None] * bi + coef_im[..., None] * br
    eye = jnp.eye(GROUPS_PER_BLOCK, dtype=F32)

    def block_diag_in(t):
        t = t.reshape(2, N_GROUP_BLOCKS, GROUPS_PER_BLOCK, SSM_STATE, SSM_CH)
        return jnp.einsum("dbgsp,gh->dbgphs", t, eye).reshape(2, N_GROUP_BLOCKS, LANES, STATE_LANES)

    taps = []
    p_re, p_im = jnp.ones_like(a_re), jnp.zeros_like(a_re)
    for _ in range(ntap):
        k_re = p_re[..., None] * t_re - p_im[..., None] * t_im
        k_im = p_re[..., None] * t_im + p_im[..., None] * t_re
        taps.append(jnp.concatenate([block_diag_in(k_re), block_diag_in(k_im)], axis=-1))
        p_re, p_im = p_re * a_re - p_im * a_im, p_re * a_im + p_im * a_re
    w_b = jnp.concatenate(taps, axis=2).astype(BF)

    def block_diag_out(c):
        c = c.astype(F32).reshape(2, N_GROUP_BLOCKS, GROUPS_PER_BLOCK, SSM_CH, SSM_STATE)
        return jnp.einsum("dbgps,gh->dbgshp", c, eye).reshape(2, N_GROUP_BLOCKS, STATE_LANES, LANES)

    w_c = jnp.concatenate([block_diag_out(c_re), -block_diag_out(c_im)], axis=2).astype(BF)

    def lanes(z):
        return z.reshape(2, N_GROUP_BLOCKS, 1, STATE_LANES)

    coef = jnp.concatenate([lanes(p_re), lanes(p_im), lanes(a_re), lanes(a_im)], axis=2)
    wg = w_glu.astype(F32).reshape(N_GROUP_BLOCKS, GROUPS_PER_BLOCK, SSM_CH, SSM_CH)
    w_g = jnp.einsum("bgpq,gh->bgphq", wg, eye).reshape(N_GROUP_BLOCKS, LANES, LANES).astype(BF)
    return w_b, w_c, coef, d.astype(F32).reshape(1, SSM_WIDTH), w_g


def _ssm_kernel(u_ref, wb_ref, wc_ref, coef_ref, h0re_ref, h0im_ref, d_ref, wg_ref,
                y_ref, fre_ref, fim_ref, ybwd, ubuf, vstate, *bufs, total, chunk, step_rows, bsz, ntap):
    bu = (bufs[0:2], bufs[2:4])
    hbf = (bufs[4:6], bufs[6:8])
    n_chunks = total // chunk
    n_steps = chunk // step_rows
    pack_steps = 2 * SUBLANES // step_rows if step_rows < 2 * SUBLANES else 1
    row = lax.broadcasted_iota(jnp.int32, (step_rows, STATE_LANES), 0)

    def project(d, c, slot):
        r0 = pl.multiple_of(c * chunk, chunk)
        u = u_ref[0, pl.ds(r0, chunk), :]
        if ntap > 1:
            if d == 0:
                ubuf[d, SUBLANES:SUBLANES + chunk] = u
            else:
                ubuf[d, 0:chunk] = u
            shifted = ubuf[d, bsz:bsz + chunk]
            if d == 0:
                ubuf[d, 0:SUBLANES] = u[chunk - SUBLANES:chunk]
            else:
                ubuf[d, chunk:chunk + SUBLANES] = u[0:SUBLANES]
            lhs = jnp.concatenate([u, shifted], axis=1)
        else:
            lhs = u
        bu[slot][d][...] = _dot(lhs.astype(BF), wb_ref[d, 0])

    def scan(slot, first):
        ar, ai, v_r, v_i, e_re, e_im = [], [], [], [], [], []
        for d in range(2):
            coef = coef_ref[d, 0]
            ar.append(jnp.broadcast_to(coef[0:1], (step_rows, STATE_LANES)))
            ai.append(jnp.broadcast_to(coef[1:2], (step_rows, STATE_LANES)))
            v_r.append(vstate[d, 0])
            v_i.append(vstate[d, 1])
            if ntap > 1:
                h0r, h0i = h0re_ref[d, 0], h0im_ref[d, 0]
                a_re, a_im = coef[2:3], coef[3:4]
                older = (row < bsz) if d == 0 else (row >= bsz)
                e_re.append(jnp.where(older, a_re * h0r - a_im * h0i, 0.0) * first)
                e_im.append(jnp.where(older, a_re * h0i + a_im * h0r, 0.0) * first)
        held = [[], []]
        for k in range(n_steps):
            for d in range(2):
                s = k if d == 0 else n_steps - 1 - k
                rows = slice(s * step_rows, (s + 1) * step_rows)
                b_r = bu[slot][d][rows, 0:STATE_LANES]
                b_i = bu[slot][d][rows, STATE_LANES:2 * STATE_LANES]
                if ntap > 1 and k == 0:
                    b_r, b_i = b_r + e_re[d], b_i + e_im[d]
                v_r[d], v_i[d] = ar[d] * v_r[d] - ai[d] * v_i[d] + b_r, ar[d] * v_i[d] + ai[d] * v_r[d] + b_i
                held[d].append((s, v_r[d], v_i[d]))
                if len(held[d]) == pack_steps:
                    hs = sorted(held[d], key=lambda h: h[0])
                    r16 = slice(hs[0][0] * step_rows, (hs[-1][0] + 1) * step_rows)
                    hbf[slot][d][r16, 0:STATE_LANES] = jnp.concatenate([h[1] for h in hs], axis=0).astype(BF)
                    hbf[slot][d][r16, STATE_LANES:2 * STATE_LANES] = jnp.concatenate(
                        [h[2] for h in hs], axis=0).astype(BF)
                    held[d] = []
        for d in range(2):
            vstate[d, 0] = v_r[d]
            vstate[d, 1] = v_i[d]

    def readout(slot, c):
        for d in range(2):
            y_part = _dot(hbf[slot][d][...], wc_ref[d, 0])
            cc = c if d == 0 else n_chunks - 1 - c
            r0 = pl.multiple_of(cc * chunk, chunk)
            if d == 0:
                y_ref[0, pl.ds(r0, chunk), :] = y_part
            else:
                ybwd[pl.ds(r0, chunk), :] = y_part

    for d in range(2):
        h0r, h0i = h0re_ref[d, 0], h0im_ref[d, 0]
        if ntap > 1:
            newer = (row >= bsz) if d == 0 else (row < bsz)
            vstate[d, 0] = jnp.where(newer, h0r, 0.0)
            vstate[d, 1] = jnp.where(newer, h0i, 0.0)
        else:
            vstate[d, 0] = h0r
            vstate[d, 1] = h0i
    if ntap > 1:
        ubuf[...] = jnp.zeros_like(ubuf)
    for d in range(2):
        hbf[1][d][...] = jnp.zeros((chunk, 2 * STATE_LANES), BF)
    project(0, 0, 0)
    project(1, n_chunks - 1, 0)

    def body(pair, carry):
        for slot in range(2):
            c = 2 * pair + slot
            first = jnp.where(c == 0, 1.0, 0.0)
            project(0, jnp.minimum(c + 1, n_chunks - 1), 1 - slot)
            project(1, jnp.maximum(n_chunks - 2 - c, 0), 1 - slot)
            readout(1 - slot, jnp.maximum(c - 1, 0))
            scan(slot, first)
        return carry

    lax.fori_loop(0, n_chunks // 2, body, 0)
    readout(1, n_chunks - 1)
    for d in range(2):
        fre_ref[d, 0] = vstate[d, 0]
        fim_ref[d, 0] = vstate[d, 1]

    fin_rows = 4 * chunk

    def finish(c, carry):
        r0 = pl.multiple_of(c * fin_rows, fin_rows)
        rows = pl.ds(r0, fin_rows)
        y = y_ref[0, rows, :] + ybwd[rows, :] + u_ref[0, rows, :] * d_ref[...]
        x = jax.nn.gelu(y, approximate=True)
        y_ref[0, rows, :] = x * jax.nn.sigmoid(_dot(x.astype(BF), wg_ref[0]))
        return carry

    lax.fori_loop(0, total // fin_rows, finish, 0)


def _s5_scan(u_tm, prep, h0_re, h0_im, bsz, ntap, rows):
    w_b, w_c, coef, d_vec, w_g = prep
    total = u_tm.shape[1]
    chunk = rows
    step_rows = bsz * ntap
    kb = LANES * ntap

    dir_spec = lambda shape: pl.BlockSpec((2, 1) + shape, lambda g: (0, g, 0, 0))
    state_shape = jax.ShapeDtypeStruct((2, N_GROUP_BLOCKS, step_rows, STATE_LANES), F32)
    return pl.pallas_call(
        functools.partial(_ssm_kernel, total=total, chunk=chunk, step_rows=step_rows, bsz=bsz, ntap=ntap),
        out_shape=[jax.ShapeDtypeStruct((N_GROUP_BLOCKS, total, LANES), F32), state_shape, state_shape],
        grid=(N_GROUP_BLOCKS,),
        in_specs=[pl.BlockSpec((1, total, LANES), lambda g: (g, 0, 0)),
                  dir_spec((kb, 2 * STATE_LANES)),
                  dir_spec((2 * STATE_LANES, LANES)),
                  dir_spec((4, STATE_LANES)),
                  dir_spec((step_rows, STATE_LANES)),
                  dir_spec((step_rows, STATE_LANES)),
                  pl.BlockSpec((1, LANES), lambda g: (0, g)),
                  pl.BlockSpec((1, LANES, LANES), lambda g: (g, 0, 0))],
        out_specs=[pl.BlockSpec((1, total, LANES), lambda g: (g, 0, 0)),
                   dir_spec((step_rows, STATE_LANES)),
                   dir_spec((step_rows, STATE_LANES))],
        scratch_shapes=[pltpu.VMEM((total, LANES), F32),
                        pltpu.VMEM((2, chunk + SUBLANES, LANES), F32),
                        pltpu.VMEM((2, 2, step_rows, STATE_LANES), F32)]
        + [pltpu.VMEM((chunk, 2 * STATE_LANES), F32)] * 4
        + [pltpu.VMEM((chunk, 2 * STATE_LANES), BF)] * 4,
        compiler_params=_cparams(("arbitrary",)),
        name="s5_scan",
    )(u_tm, w_b, w_c, coef, h0_re, h0_im, d_vec, w_g)


def _merge_kernel(attn_ref, ssm_ref, x_ref, gt_ref, ga_ref, gs_ref, w_ref, o_ref, *, tile, bsz):
    bi = pl.program_id(1)
    ssm = jnp.concatenate([ssm_ref[k, pl.ds(bi, tile, stride=bsz), :] for k in range(SSM_WIDTH // LANES)], axis=1)
    na = _rms(attn_ref[0]) * ga_ref[...]
    ns = _rms(ssm) * gs_ref[...]
    mo = _dot(na.astype(BF), w_ref[0:ATTN_WIDTH, :]) + _dot(ns.astype(BF), w_ref[ATTN_WIDTH:D_MODEL, :])
    o_ref[0] = x_ref[0] + gt_ref[0] * mo


def _merge(attn, ssm, x, gate, g_attn, g_ssm, w_out, tile):
    b, n, _ = x.shape
    per_batch = gate.shape[0] > 1
    row = lambda width: pl.BlockSpec((1, tile, width), lambda t, i: (i, t, 0))
    n_slabs = SSM_WIDTH // LANES
    return pl.pallas_call(
        functools.partial(_merge_kernel, tile=tile, bsz=b),
        out_shape=jax.ShapeDtypeStruct((b, n, D_MODEL), F32),
        grid=(n // tile, b),
        in_specs=[row(ATTN_WIDTH),
                  pl.BlockSpec((n_slabs, tile * b, LANES), lambda t, i: (0, t, 0)),
                  row(D_MODEL),
                  pl.BlockSpec((1, 1, D_MODEL), (lambda t, i: (i, 0, 0)) if per_batch else (lambda t, i: (0, 0, 0))),
                  _const_spec((1, ATTN_WIDTH)), _const_spec((1, SSM_WIDTH)),
                  _const_spec((D_MODEL, D_MODEL))],
        out_specs=row(D_MODEL),
        compiler_params=_cparams(("arbitrary", "arbitrary")),
        name="merge_projection",
    )(attn, ssm, x, gate, g_attn.reshape(1, -1), g_ssm.reshape(1, -1), w_out)


HALO = 16


def _ffn_kernel(x_ref, xp_ref, xn_ref, sh_ref, sc_ref, gt_ref, g2_ref, wup_ref, cw_ref, cb_ref, wdn_ref, gf_ref,
                o_ref, hext, act, *, tile, n_tiles):
    t = pl.program_id(1)
    rows = tile + 2 * HALO

    def modulated(x):
        h = _rms(x) * g2_ref[...]
        return (h * (1.0 + sc_ref[0]) + sh_ref[0]).astype(BF)

    x = x_ref[0]
    hext[0:HALO] = jnp.where(t > 0, modulated(xp_ref[0]), jnp.zeros((HALO, D_MODEL), BF))
    hext[HALO:HALO + tile] = modulated(x)
    hext[HALO + tile:2 * HALO + tile] = jnp.where(t < n_tiles - 1, modulated(xn_ref[0]),
                                                  jnp.zeros((HALO, D_MODEL), BF))
    h_all = hext[...]
    for c in range(D_FF // FF_CHUNK):
        halves = []
        for base in (c * FF_CHUNK, D_FF + c * FF_CHUNK):
            cols = slice(base, base + FF_CHUNK)
            up = _dot(h_all, wup_ref[:, cols])
            prev = pltpu.roll(up, 1, 0)[HALO:HALO + tile]
            nxt = pltpu.roll(up, rows - 1, 0)[HALO:HALO + tile]
            conv = (prev * cw_ref[0:1, cols] + up[HALO:HALO + tile] * cw_ref[1:2, cols]
                    + nxt * cw_ref[2:3, cols] + cb_ref[:, cols])
            halves.append(conv)
        gate, val = halves
        act[:, c * FF_CHUNK:(c + 1) * FF_CHUNK] = (gate * jax.nn.sigmoid(gate) * val).astype(BF)
    y = x + gt_ref[0] * _dot(act[...], wdn_ref[...])
    o_ref[0] = _rms(y) * gf_ref[...]


def _conv_ffn(x, shift, scale, gate, g2, w_up, conv_w, conv_b, w_down, g_final, tile):
    b, n, _ = x.shape
    n_tiles = n // tile
    per = tile // HALO
    nh = n // HALO
    per_batch = shift.shape[0] > 1
    mod_spec = pl.BlockSpec((1, 1, D_MODEL), (lambda i, t: (i, 0, 0)) if per_batch else (lambda i, t: (0, 0, 0)))
    return pl.pallas_call(
        functools.partial(_ffn_kernel, tile=tile, n_tiles=n_tiles),
        out_shape=jax.ShapeDtypeStruct((b, n, D_MODEL), F32),
        grid=(b, n_tiles),
        in_specs=[pl.BlockSpec((1, tile, D_MODEL), lambda i, t: (i, t, 0)),
                  pl.BlockSpec((1, HALO, D_MODEL), lambda i, t: (i, jnp.maximum(t * per - 1, 0), 0)),
                  pl.BlockSpec((1, HALO, D_MODEL), lambda i, t: (i, jnp.minimum((t + 1) * per, nh - 1), 0)),
                  mod_spec, mod_spec, mod_spec,
                  _const_spec((1, D_MODEL)),
                  _const_spec((D_MODEL, 2 * D_FF)),
                  _const_spec((3, 2 * D_FF)),
                  _const_spec((1, 2 * D_FF)),
                  _const_spec((D_FF, D_MODEL)),
                  _const_spec((1, D_MODEL))],
        out_specs=pl.BlockSpec((1, tile, D_MODEL), lambda i, t: (i, t, 0)),
        scratch_shapes=[pltpu.VMEM((tile + 2 * HALO, D_MODEL), BF),
                        pltpu.VMEM((tile, D_FF), BF)],
        compiler_params=_cparams(("arbitrary", "arbitrary")),
        name="conv_ffn",
    )(x, x, x, shift, scale, gate, g2.reshape(1, -1), w_up, conv_w, conv_b.reshape(1, -1), w_down,
      g_final.reshape(1, -1))


def _dup_heads(w):
    a, b = w[..., :HEAD_DIM], w[..., HEAD_DIM:]
    return jnp.concatenate([a, a, b, b], axis=-1)


def _stream(x, mods, weights, rope_tabs, ctx_kv, h0, ssm_prep, with_kv, tile, ssm_rows, attn_tile):
    sh1, sc1, gt1, sh2, sc2, gt2 = mods
    (g1, g2, w_in, sink, g_attn, g_ssm, w_out, w_up, conv_w, conv_b, w_down, g_final) = weights
    b, n, _ = x.shape
    outs = _in_projection(x, sh1, sc1, g1, w_in, rope_tabs, with_kv, tile)
    q, kk, vv, u = outs[:4]
    if ctx_kv is None:
        attn = _context_attention(q, kk, vv, sink)
    else:
        attn = _latent_attention(q, kk, vv, ctx_kv[0], ctx_kv[1], sink, attn_tile)
    ntap = max(1, SUBLANES // b)
    y_tm, fin_re, fin_im = _s5_scan(u, ssm_prep[ntap], h0[0], h0[1], b, ntap, ssm_rows)
    x1 = _merge(attn, y_tm, x, gt1, g_attn, g_ssm, w_out, tile)
    y = _conv_ffn(x1, sh2, sc2, gt2, g2, w_up, conv_w, conv_b, w_down, g_final, tile)
    return y, outs[4:], (fin_re, fin_im)


def kernel(x_prompt, x_sample, cache_k, cache_v, state_ssm_re, state_ssm_im, c, c_ctx, w_ada, b_ada, g_norm1, g_norm2, w_in, attn_sink, ssm_lam_re, ssm_lam_im, ssm_log_dt, ssm_b_re, ssm_b_im, ssm_c_re, ssm_c_im, ssm_d, ssm_w_glu, g_out_attn, g_out_ssm, w_out, w_up, conv_w, conv_b, w_down, g_final):
    depth = w_in.shape[0]
    assert depth == 1
    l = 0
    bp, seq, _ = x_prompt.shape
    bs, dec_seq, _ = x_sample.shape
    past = cache_k.shape[2]

    cond = jnp.concatenate([c_ctx[None, :], c, jnp.zeros((SUBLANES - 1 - bs, D_MODEL), F32)], axis=0)
    mod = _adaln(cond, w_ada[l], b_ada[l])
    mods_ctx = [mod[0:1, i * D_MODEL:(i + 1) * D_MODEL].reshape(1, 1, D_MODEL) for i in range(6)]
    mods_lat = [mod[1:1 + bs, i * D_MODEL:(i + 1) * D_MODEL].reshape(bs, 1, D_MODEL) for i in range(6)]

    wi = w_in[l]
    wq = wi[:, :ATTN_WIDTH]
    wk = wi[:, ATTN_WIDTH:ATTN_WIDTH + KV_WIDTH]
    wv = wi[:, ATTN_WIDTH + KV_WIDTH:ATTN_WIDTH + 2 * KV_WIDTH]
    wu = wi[:, ATTN_WIDTH + 2 * KV_WIDTH:]
    w_in_lat = jnp.concatenate([wq, _dup_heads(wk), _dup_heads(wv), wu], axis=1).astype(BF)
    w_in_ctx = jnp.concatenate([w_in_lat, wk.astype(BF), wv.astype(BF)], axis=1)
    weights = (g_norm1[l], g_norm2[l], None, attn_sink[l], g_out_attn[l], g_out_ssm[l], w_out[l].astype(BF),
               w_up[l].astype(BF), conv_w[l], conv_b[l], w_down[l].astype(BF), g_final)

    ssm_params = (ssm_lam_re[l], ssm_lam_im[l], ssm_log_dt[l], ssm_b_re[l], ssm_b_im[l],
                  ssm_c_re[l], ssm_c_im[l], ssm_d[l], ssm_w_glu[l])
    taps = {max(1, SUBLANES // bp), max(1, SUBLANES // bs)}
    ssm_prep = {nt: _ssm_prepare(*ssm_params, nt) for nt in taps}

    zeros_h0 = jnp.zeros((2, N_GROUP_BLOCKS, bp, STATE_LANES), F32)
    w_ctx = weights[:2] + (w_in_ctx,) + weights[3:]
    y_prompt, (k_new, v_new), (fin_re, fin_im) = _stream(
        x_prompt, mods_ctx, w_ctx, None, None, (zeros_h0, zeros_h0), ssm_prep, True,
        tile=seq, ssm_rows=256, attn_tile=None)
    new_k = k_new.reshape(bp, 1, seq, KV_HEADS, HEAD_DIM)
    new_v = v_new.reshape(bp, 1, seq, KV_HEADS, HEAD_DIM)

    def states_out(fin):
        return fin.transpose(2, 0, 1, 3).reshape(bp, 1, 2, SSM_GROUPS, SSM_STATE)

    ckk = _dup_heads(cache_k[:, l].reshape(bs, past, KV_WIDTH)).astype(BF)
    cvv = _dup_heads(cache_v[:, l].reshape(bs, past, KV_WIDTH)).astype(BF)

    def h0_in(state):
        s = state.astype(F32).reshape(bs, 2, N_GROUP_BLOCKS, STATE_LANES).transpose(1, 2, 0, 3)
        return jnp.concatenate([s] * max(1, SUBLANES // bs), axis=2)

    w_lat = weights[:2] + (w_in_lat,) + weights[3:]
    y_sample, _, _ = _stream(
        x_sample, mods_lat, w_lat, _rope_tables(dec_seq), (ckk, cvv),
        (h0_in(state_ssm_re[:, l]), h0_in(state_ssm_im[:, l])), ssm_prep, False,
        tile=512, ssm_rows=256, attn_tile=512)

    return (y_prompt, y_sample, new_k, new_v, states_out(fin_re), states_out(fin_im))
```

```python
import functools
import math

import jax
import jax.numpy as jnp
from jax import lax
from jax.experimental import pallas as pl
from jax.experimental.pallas import tpu as pltpu

D_MODEL = 1024
HEAD_DIM = 64
N_HEADS = 8
KV_HEADS = 2
ATTN_WIDTH = N_HEADS * HEAD_DIM
KV_WIDTH = KV_HEADS * HEAD_DIM
BLOCK = 128
GRID_W = 64
ROPE_THETA = 10000.0
ROPE_FREQS = HEAD_DIM // 4
SSM_CH = 16
SSM_WIDTH = D_MODEL - ATTN_WIDTH
SSM_GROUPS = SSM_WIDTH // SSM_CH
SSM_STATE = 64
D_FF = 2816
EPS = 1e-6
NEG_INF = -1e30

LANES = 128
SUBLANES = 8
GROUPS_PER_BLOCK = LANES // SSM_CH
N_GROUP_BLOCKS = SSM_GROUPS // GROUPS_PER_BLOCK
STATE_LANES = GROUPS_PER_BLOCK * SSM_STATE
FF_CHUNK = 256
VMEM_LIMIT = 56 * 1024 * 1024

BF = jnp.bfloat16
F32 = jnp.float32


def _cparams(sem):
    return pltpu.CompilerParams(dimension_semantics=sem, vmem_limit_bytes=VMEM_LIMIT)


def _const_spec(shape):
    nd = len(shape)
    return pl.BlockSpec(shape, lambda *_: (0,) * nd, pipeline_mode=pl.Buffered(1))


def _rms(x):
    return x * lax.rsqrt(jnp.mean(x * x, axis=-1, keepdims=True) + EPS)


def _dot(a, b):
    return jnp.dot(a, b, preferred_element_type=F32)


def _dot_nt(a, b):
    return lax.dot_general(a, b, (((1,), (1,)), ((), ())), preferred_element_type=F32)


def _ada_kernel(cond_ref, w_ref, b_ref, o_ref):
    c = cond_ref[...]
    s = c * jax.nn.sigmoid(c)
    o_ref[...] = _dot(s.astype(BF), w_ref[...].astype(BF)) + b_ref[...]


def _adaln(cond, w_ada, b_ada):
    n = cond.shape[0]
    return pl.pallas_call(
        _ada_kernel,
        out_shape=jax.ShapeDtypeStruct((n, 6 * D_MODEL), F32),
        grid=(6,),
        in_specs=[pl.BlockSpec((n, D_MODEL), lambda i: (0, 0)),
                  pl.BlockSpec((D_MODEL, D_MODEL), lambda i: (0, i)),
                  pl.BlockSpec((1, D_MODEL), lambda i: (0, i))],
        out_specs=pl.BlockSpec((n, D_MODEL), lambda i: (0, i)),
        compiler_params=_cparams(("arbitrary",)),
        name="adaln",
    )(cond, w_ada, b_ada.reshape(1, -1))


def _rope(z, cos, sin_lo, sin_hi):
    return z * cos + pltpu.roll(z, LANES - ROPE_FREQS, 1) * sin_lo + pltpu.roll(z, ROPE_FREQS, 1) * sin_hi


def _inproj_kernel(*refs, rope, with_kv, tile, bsz):
    x_ref, sh_ref, sc_ref, g_ref, w_ref = refs[:5]
    pos = 5
    if rope:
        cos_ref, slo_ref, shi_ref = refs[pos:pos + 3]
        pos += 3
    q_ref, kk_ref, vv_ref, u_ref = refs[pos:pos + 4]
    pos += 4
    h = _rms(x_ref[0]) * g_ref[...]
    h = h * (1.0 + sc_ref[0]) + sh_ref[0]
    proj = _dot(h.astype(BF), w_ref[...])
    q = proj[:, :ATTN_WIDTH] * (HEAD_DIM ** -0.5)
    kk = proj[:, ATTN_WIDTH:ATTN_WIDTH + 2 * KV_WIDTH]
    vv = proj[:, ATTN_WIDTH + 2 * KV_WIDTH:ATTN_WIDTH + 4 * KV_WIDTH]
    u0 = ATTN_WIDTH + 4 * KV_WIDTH
    bi = pl.program_id(1)
    for k in range(SSM_WIDTH // LANES):
        u_ref[k, pl.ds(bi, tile, stride=bsz), :] = proj[:, u0 + k * LANES:u0 + (k + 1) * LANES]
    vv_ref[0] = vv.astype(BF)
    if rope:
        cos, slo, shi = cos_ref[...], slo_ref[...], shi_ref[...]
        for c in range(ATTN_WIDTH // LANES):
            q_ref[0, :, c * LANES:(c + 1) * LANES] = _rope(
                q[:, c * LANES:(c + 1) * LANES], cos, slo, shi).astype(BF)
        for c in range(2 * KV_WIDTH // LANES):
            kk_ref[0, :, c * LANES:(c + 1) * LANES] = _rope(
                kk[:, c * LANES:(c + 1) * LANES], cos, slo, shi).astype(BF)
    else:
        q_ref[0] = q.astype(BF)
        kk_ref[0] = kk.astype(BF)
    if with_kv:
        k_ref, v_ref = refs[pos:pos + 2]
        k0 = u0 + SSM_WIDTH
        k_ref[0] = proj[:, k0:k0 + KV_WIDTH]
        v_ref[0] = proj[:, k0 + KV_WIDTH:k0 + 2 * KV_WIDTH]


def _in_projection(x, shift, scale, g, w, rope_tabs, with_kv, tile):
    b, n, _ = x.shape
    nc = w.shape[1]
    per_batch = shift.shape[0] > 1
    mod_spec = pl.BlockSpec((1, 1, D_MODEL), (lambda t, i: (i, 0, 0)) if per_batch else (lambda t, i: (0, 0, 0)))
    in_specs = [pl.BlockSpec((1, tile, D_MODEL), lambda t, i: (i, t, 0)),
                mod_spec, mod_spec,
                _const_spec((1, D_MODEL)),
                _const_spec((D_MODEL, nc))]
    args = [x, shift, scale, g.reshape(1, -1), w]
    if rope_tabs is not None:
        in_specs += [pl.BlockSpec((tile, LANES), lambda t, i: (t, 0))] * 3
        args += list(rope_tabs)

    def out(width, dtype):
        return (jax.ShapeDtypeStruct((b, n, width), dtype),
                pl.BlockSpec((1, tile, width), lambda t, i: (i, t, 0)))

    n_slabs = SSM_WIDTH // LANES
    u_out = (jax.ShapeDtypeStruct((n_slabs, n * b, LANES), F32),
             pl.BlockSpec((n_slabs, tile * b, LANES), lambda t, i: (0, t, 0)))
    outs = [out(ATTN_WIDTH, BF), out(2 * KV_WIDTH, BF), out(2 * KV_WIDTH, BF), u_out]
    if with_kv:
        outs += [out(KV_WIDTH, F32), out(KV_WIDTH, F32)]
    return pl.pallas_call(
        functools.partial(_inproj_kernel, rope=rope_tabs is not None, with_kv=with_kv, tile=tile, bsz=b),
        out_shape=[o[0] for o in outs],
        grid=(n // tile, b),
        in_specs=in_specs,
        out_specs=[o[1] for o in outs],
        compiler_params=_cparams(("arbitrary", "arbitrary")),
        name="in_projection",
    )(*args)


def _rope_tables(n_tokens):
    t = jnp.arange(n_tokens, dtype=F32)
    row = jnp.floor(t / GRID_W)
    col = t - row * GRID_W
    inv_freq = ROPE_THETA ** (-jnp.arange(ROPE_FREQS, dtype=F32) / ROPE_FREQS)
    ang_row = row[:, None] * inv_freq[None, :]
    ang_col = col[:, None] * inv_freq[None, :]
    ang = jnp.concatenate([ang_row, ang_row, ang_col, ang_col] * (LANES // HEAD_DIM), axis=-1)
    first = (jnp.arange(LANES) % (2 * ROPE_FREQS)) < ROPE_FREQS
    sin = jnp.sin(ang)
    return jnp.cos(ang), jnp.where(first, -sin, 0.0), jnp.where(first, 0.0, sin)


def _stack_heads(q_rows, j, lo):
    qa = q_rows[:, (2 * j) * LANES:(2 * j + 1) * LANES]
    qb = q_rows[:, (2 * j + 1) * LANES:(2 * j + 2) * LANES]
    keep_lo = jnp.where(lo, 1.0, 0.0).astype(BF)
    keep_hi = jnp.where(lo, 0.0, 1.0).astype(BF)
    return jnp.concatenate([qa * keep_lo, qa * keep_hi, qb * keep_lo, qb * keep_hi], axis=0)


def _sink_column(sink_ref, j, t):
    return jnp.concatenate([jnp.full((t, 1), sink_ref[4 * j + a], F32) for a in range(4)], axis=0)


def _softmax_pv(scores, values, sink_col):
    m = sink_col
    for s in scores:
        m = jnp.maximum(m, jnp.max(s, axis=-1, keepdims=True))
    den = jnp.exp(sink_col - m)
    acc = None
    for s, v in zip(scores, values):
        p = jnp.exp(s - m)
        den = den + jnp.sum(p, axis=-1, keepdims=True)
        o = _dot(p.astype(BF), v)
        acc = o if acc is None else acc + o
    return acc / den


def _unstack_store(o_ref, o, rows, j, lo, t):
    o_ref[0, rows, (2 * j) * LANES:(2 * j + 1) * LANES] = jnp.where(lo, o[0:t], o[t:2 * t])
    o_ref[0, rows, (2 * j + 1) * LANES:(2 * j + 2) * LANES] = jnp.where(lo, o[2 * t:3 * t], o[3 * t:4 * t])


def _ctx_attn_kernel(sink_ref, q_ref, kk_ref, vv_ref, o_ref, *, n):
    lo = lax.broadcasted_iota(jnp.int32, (BLOCK, LANES), 1) < HEAD_DIM
    for sub in range(n // BLOCK):
        rows = slice(sub * BLOCK, (sub + 1) * BLOCK)
        q_rows = q_ref[0, rows, :]
        for j in range(KV_HEADS):
            cols = slice(j * LANES, (j + 1) * LANES)
            qs = _stack_heads(q_rows, j, lo)
            s = _dot_nt(qs, kk_ref[0, :, cols])
            o = _softmax_pv([s], [vv_ref[0, :, cols]], _sink_column(sink_ref, j, BLOCK))
            _unstack_store(o_ref, o, rows, j, lo, BLOCK)


def _context_attention(q, kk, vv, sink):
    b, n, _ = q.shape
    return pl.pallas_call(
        functools.partial(_ctx_attn_kernel, n=n),
        out_shape=jax.ShapeDtypeStruct((b, n, ATTN_WIDTH), F32),
        grid=(b,),
        in_specs=[pl.BlockSpec(memory_space=pltpu.SMEM),
                  pl.BlockSpec((1, n, ATTN_WIDTH), lambda i: (i, 0, 0)),
                  pl.BlockSpec((1, n, 2 * KV_WIDTH), lambda i: (i, 0, 0)),
                  pl.BlockSpec((1, n, 2 * KV_WIDTH), lambda i: (i, 0, 0))],
        out_specs=pl.BlockSpec((1, n, ATTN_WIDTH), lambda i: (i, 0, 0)),
        compiler_params=_cparams(("arbitrary",)),
        name="context_attention",
    )(sink, q, kk, vv)


def _lat_attn_kernel(sink_ref, q_ref, kkm_ref, kkp_ref, kkn_ref, vvm_ref, vvp_ref, vvn_ref,
                     ckk_ref, cvv_ref, o_ref, kext, vext, *, tq, n):
    t = pl.program_id(1)
    kext[0:BLOCK] = kkp_ref[0]
    kext[BLOCK:BLOCK + tq] = kkm_ref[0]
    kext[BLOCK + tq:2 * BLOCK + tq] = kkn_ref[0]
    vext[0:BLOCK] = vvp_ref[0]
    vext[BLOCK:BLOCK + tq] = vvm_ref[0]
    vext[BLOCK + tq:2 * BLOCK + tq] = vvn_ref[0]
    lo = lax.broadcasted_iota(jnp.int32, (BLOCK, LANES), 1) < HEAD_DIM
    r = lax.broadcasted_iota(jnp.int32, (BLOCK, 3 * BLOCK), 0)
    w = lax.broadcasted_iota(jnp.int32, (BLOCK, 3 * BLOCK), 1)
    band = (w >= r) & (w <= r + 2 * BLOCK)
    for sub in range(tq // BLOCK):
        rows = slice(sub * BLOCK, (sub + 1) * BLOCK)
        key = t * tq + (sub - 1) * BLOCK + w
        valid = jnp.where(band & (key >= 0) & (key < n), 1.0, 0.0)
        valid4 = jnp.concatenate([valid] * 4, axis=0) > 0.5
        q_rows = q_ref[0, rows, :]
        win = slice(sub * BLOCK, (sub + 3) * BLOCK)
        for j in range(KV_HEADS):
            cols = slice(j * LANES, (j + 1) * LANES)
            qs = _stack_heads(q_rows, j, lo)
            s_loc = jnp.where(valid4, _dot_nt(qs, kext[win, cols]), NEG_INF)
            s_ctx = _dot_nt(qs, ckk_ref[0, :, cols])
            o = _softmax_pv([s_loc, s_ctx], [vext[win, cols], cvv_ref[0, :, cols]],
                            _sink_column(sink_ref, j, BLOCK))
            _unstack_store(o_ref, o, rows, j, lo, BLOCK)


def _latent_attention(q, kk, vv, ckk, cvv, sink, tq):
    b, n, _ = q.shape
    past = ckk.shape[1]
    nblk = n // BLOCK
    per = tq // BLOCK
    kw = 2 * KV_WIDTH
    main = pl.BlockSpec((1, tq, kw), lambda i, t: (i, t, 0))
    prev = pl.BlockSpec((1, BLOCK, kw), lambda i, t: (i, jnp.maximum(t * per - 1, 0), 0))
    nxt = pl.BlockSpec((1, BLOCK, kw), lambda i, t: (i, jnp.minimum((t + 1) * per, nblk - 1), 0))
    ctx = pl.BlockSpec((1, past, kw), lambda i, t: (i, 0, 0))
    return pl.pallas_call(
        functools.partial(_lat_attn_kernel, tq=tq, n=n),
        out_shape=jax.ShapeDtypeStruct((b, n, ATTN_WIDTH), F32),
        grid=(b, n // tq),
        in_specs=[pl.BlockSpec(memory_space=pltpu.SMEM),
                  pl.BlockSpec((1, tq, ATTN_WIDTH), lambda i, t: (i, t, 0)),
                  main, prev, nxt, main, prev, nxt, ctx, ctx],
        out_specs=pl.BlockSpec((1, tq, ATTN_WIDTH), lambda i, t: (i, t, 0)),
        scratch_shapes=[pltpu.VMEM((tq + 2 * BLOCK, kw), BF), pltpu.VMEM((tq + 2 * BLOCK, kw), BF)],
        compiler_params=_cparams(("arbitrary", "arbitrary")),
        name="latent_attention",
    )(sink, q, kk, kk, kk, vv, vv, vv, ckk, cvv)


def _ssm_prepare(lam_re, lam_im, log_dt, b_re, b_im, c_re, c_im, d, w_glu, ntap):
    lam_re, lam_im = lam_re.astype(F32), lam_im.astype(F32)
    dt = jnp.exp(log_dt.astype(F32))[..., None]
    mag = jnp.exp(lam_re * dt)
    ang = lam_im * dt
    a_re, a_im = mag * jnp.cos(ang), mag * jnp.sin(ang)
    den = lam_re * lam_re + lam_im * lam_im
    num_re = a_re - 1.0
    coef_re = (num_re * lam_re + a_im * lam_im) / den
    coef_im = (a_im * lam_re - num_re * lam_im) / den
    br, bi = b_re.astype(F32), b_im.astype(F32)
    t_re = coef_re[..., None] * br - coef_im[..., None] * bi
    t_im = coef_re[..., None] * bi + coef_im[..., None] * br
    eye = jnp.eye(GROUPS_PER_BLOCK, dtype=F32)

    def block_diag_in(t):
        t = t.reshape(2, N_GROUP_BLOCKS, GROUPS_PER_BLOCK, SSM_STATE, SSM_CH)
        return jnp.einsum("dbgsp,gh->dbgphs", t, eye).reshape(2, N_GROUP_BLOCKS, LANES, STATE_LANES)

    taps = []
    p_re, p_im = jnp.ones_like(a_re), jnp.zeros_like(a_re)
    for _ in range(ntap):
        k_re = p_re[..., None] * t_re - p_im[..., None] * t_im
        k_im = p_re[..., None] * t_im + p_im[..., None] * t_re
        taps.append(jnp.concatenate([block_diag_in(k_re), block_diag_in(k_im)], axis=-1))
        p_re, p_im = p_re * a_re - p_im * a_im, p_re * a_im + p_im * a_re
    w_b = jnp.concatenate(taps, axis=2).astype(BF)

    def block_diag_out(c):
        c = c.astype(F32).reshape(2, N_GROUP_BLOCKS, GROUPS_PER_BLOCK, SSM_CH, SSM_STATE)
        return jnp.einsum("dbgps,gh->dbgshp", c, eye).reshape(2, N_GROUP_BLOCKS, STATE_LANES, LANES)

    w_c = jnp.concatenate([block_diag_out(c_re), -block_diag_out(c_im)], axis=2).astype(BF)

    def lanes(z):
        return z.reshape(2, N_GROUP_BLOCKS, 1, STATE_LANES)

    coef = jnp.concatenate([lanes(p_re), lanes(p_im), lanes(a_re), lanes(a_im)], axis=2)
    wg = w_glu.astype(F32).reshape(N_GROUP_BLOCKS, GROUPS_PER_BLOCK, SSM_CH, SSM_CH)
    w_g = jnp.einsum("bgpq,gh->bgphq", wg, eye).reshape(N_GROUP_BLOCKS, LANES, LANES).astype(BF)
    return w_b, w_c, coef, d.astype(F32).reshape(1, SSM_WIDTH), w_g


def _ssm_kernel(u_ref, wb_ref, wc_ref, coef_ref, h0re_ref, h0im_ref, d_ref, wg_ref,
                y_ref, fre_ref, fim_ref, ybwd, ubuf, vstate, *bufs, total, chunk, step_rows, bsz, ntap):
    bu = (bufs[0:2], bufs[2:4])
    hbf = (bufs[4:6], bufs[6:8])
    n_chunks = total // chunk
    n_steps = chunk // step_rows
    pack_steps = 2 * SUBLANES // step_rows if step_rows < 2 * SUBLANES else 1
    row = lax.broadcasted_iota(jnp.int32, (step_rows, STATE_LANES), 0)

    def project(d, c, slot):
        r0 = pl.multiple_of(c * chunk, chunk)
        u = u_ref[0, pl.ds(r0, chunk), :]
        if ntap > 1:
            if d == 0:
                ubuf[d, SUBLANES:SUBLANES + chunk] = u
            else:
                ubuf[d, 0:chunk] = u
            shifted = ubuf[d, bsz:bsz + chunk]
            if d == 0:
                ubuf[d, 0:SUBLANES] = u[chunk - SUBLANES:chunk]
            else:
                ubuf[d, chunk:chunk + SUBLANES] = u[0:SUBLANES]
            lhs = jnp.concatenate([u, shifted], axis=1)
        else:
            lhs = u
        bu[slot][d][...] = _dot(lhs.astype(BF), wb_ref[d, 0])

    def scan(slot, first):
        ar, ai, v_r, v_i, e_re, e_im = [], [], [], [], [], []
        for d in range(2):
            coef = coef_ref[d, 0]
            ar.append(jnp.broadcast_to(coef[0:1], (step_rows, STATE_LANES)))
            ai.append(jnp.broadcast_to(coef[1:2], (step_rows, STATE_LANES)))
            v_r.append(vstate[d, 0])
            v_i.append(vstate[d, 1])
            if ntap > 1:
                h0r, h0i = h0re_ref[d, 0], h0im_ref[d, 0]
                a_re, a_im = coef[2:3], coef[3:4]
                older = (row < bsz) if d == 0 else (row >= bsz)
                e_re.append(jnp.where(older, a_re * h0r - a_im * h0i, 0.0) * first)
                e_im.append(jnp.where(older, a_re * h0i + a_im * h0r, 0.0) * first)
        held = [[], []]
        for k in range(n_steps):
            for d in range(2):
                s = k if d == 0 else n_steps - 1 - k
                rows = slice(s * step_rows, (s + 1) * step_rows)
                b_r = bu[slot][d][rows, 0:STATE_LANES]
                b_i = bu[slot][d][rows, STATE_LANES:2 * STATE_LANES]
                if ntap > 1 and k == 0:
                    b_r, b_i = b_r + e_re[d], b_i + e_im[d]
                v_r[d], v_i[d] = ar[d] * v_r[d] - ai[d] * v_i[d] + b_r, ar[d] * v_i[d] + ai[d] * v_r[d] + b_i
                held[d].append((s, v_r[d], v_i[d]))
                if len(held[d]) == pack_steps:
                    hs = sorted(held[d], key=lambda h: h[0])
                    r16 = slice(hs[0][0] * step_rows, (hs[-1][0] + 1) * step_rows)
                    hbf[slot][d][r16, 0:STATE_LANES] = jnp.concatenate([h[1] for h in hs], axis=0).astype(BF)
                    hbf[slot][d][r16, STATE_LANES:2 * STATE_LANES] = jnp.concatenate(
                        [h[2] for h in hs], axis=0).astype(BF)
                    held[d] = []
        for d in range(2):
            vstate[d, 0] = v_r[d]
            vstate[d, 1] = v_i[d]

    def readout(slot, c):
        for d in range(2):
            y_part = _dot(hbf[slot][d][...], wc_ref[d, 0])
            cc = c if d == 0 else n_chunks - 1 - c
            r0 = pl.multiple_of(cc * chunk, chunk)
            if d == 0:
                y_ref[0, pl.ds(r0, chunk), :] = y_part
            else:
                ybwd[pl.ds(r0, chunk), :] = y_part

    for d in range(2):
        h0r, h0i = h0re_ref[d, 0], h0im_ref[d, 0]
        if ntap > 1:
            newer = (row >= bsz) if d == 0 else (row < bsz)
            vstate[d, 0] = jnp.where(newer, h0r, 0.0)
            vstate[d, 1] = jnp.where(newer, h0i, 0.0)
        else:
            vstate[d, 0] = h0r
            vstate[d, 1] = h0i
    if ntap > 1:
        ubuf[...] = jnp.zeros_like(ubuf)
    for d in range(2):
        hbf[1][d][...] = jnp.zeros((chunk, 2 * STATE_LANES), BF)
    project(0, 0, 0)
    project(1, n_chunks - 1, 0)

    def body(pair, carry):
        for slot in range(2):
            c = 2 * pair + slot
            first = jnp.where(c == 0, 1.0, 0.0)
            project(0, jnp.minimum(c + 1, n_chunks - 1), 1 - slot)
            project(1, jnp.maximum(n_chunks - 2 - c, 0), 1 - slot)
            readout(1 - slot, jnp.maximum(c - 1, 0))
            scan(slot, first)
        return carry

    lax.fori_loop(0, n_chunks // 2, body, 0)
    readout(1, n_chunks - 1)
    for d in range(2):
        fre_ref[d, 0] = vstate[d, 0]
        fim_ref[d, 0] = vstate[d, 1]

    fin_rows = 4 * chunk

    def finish(c, carry):
        r0 = pl.multiple_of(c * fin_rows, fin_rows)
        rows = pl.ds(r0, fin_rows)
        y = y_ref[0, rows, :] + ybwd[rows, :] + u_ref[0, rows, :] * d_ref[...]
        x = jax.nn.gelu(y, approximate=True)
        y_ref[0, rows, :] = x * jax.nn.sigmoid(_dot(x.astype(BF), wg_ref[0]))
        return carry

    lax.fori_loop(0, total // fin_rows, finish, 0)


def _s5_scan(u_tm, prep, h0_re, h0_im, bsz, ntap, rows):
    w_b, w_c, coef, d_vec, w_g = prep
    total = u_tm.shape[1]
    chunk = rows
    step_rows = bsz * ntap
    kb = LANES * ntap

    dir_spec = lambda shape: pl.BlockSpec((2, 1) + shape, lambda g: (0, g, 0, 0))
    state_shape = jax.ShapeDtypeStruct((2, N_GROUP_BLOCKS, step_rows, STATE_LANES), F32)
    return pl.pallas_call(
        functools.partial(_ssm_kernel, total=total, chunk=chunk, step_rows=step_rows, bsz=bsz, ntap=ntap),
        out_shape=[jax.ShapeDtypeStruct((N_GROUP_BLOCKS, total, LANES), F32), state_shape, state_shape],
        grid=(N_GROUP_BLOCKS,),
        in_specs=[pl.BlockSpec((1, total, LANES), lambda g: (g, 0, 0)),
                  dir_spec((kb, 2 * STATE_LANES)),
                  dir_spec((2 * STATE_LANES, LANES)),
                  dir_spec((4, STATE_LANES)),
                  dir_spec((step_rows, STATE_LANES)),
                  dir_spec((step_rows, STATE_LANES)),
                  pl.BlockSpec((1, LANES), lambda g: (0, g)),
                  pl.BlockSpec((1, LANES, LANES), lambda g: (g, 0, 0))],
        out_specs=[pl.BlockSpec((1, total, LANES), lambda g: (g, 0, 0)),
                   dir_spec((step_rows, STATE_LANES)),
                   dir_spec((step_rows, STATE_LANES))],
        scratch_shapes=[pltpu.VMEM((total, LANES), F32),
                        pltpu.VMEM((2, chunk + SUBLANES, LANES), F32),
                        pltpu.VMEM((2, 2, step_rows, STATE_LANES), F32)]
        + [pltpu.VMEM((chunk, 2 * STATE_LANES), F32)] * 4
        + [pltpu.VMEM((chunk, 2 * STATE_LANES), BF)] * 4,
        compiler_params=_cparams(("arbitrary",)),
        name="s5_scan",
    )(u_tm, w_b, w_c, coef, h0_re, h0_im, d_vec, w_g)


HALO = 16


def _merge_ffn_kernel(a_ref, ap_ref, an_ref, s_ref, sp_ref, sn_ref, x_ref, xp_ref, xn_ref,
                      gt1_ref, ga_ref, gs_ref, wo_ref,
                      sh_ref, sc_ref, gt_ref, g2_ref, wup_ref, cw_ref, cb_ref, wdn_ref, gf_ref,
                      o_ref, naext, nsext, xext, hext, act, *, tile, n_tiles, bsz):
    t = pl.program_id(0)
    bi = pl.program_id(1)
    rows = tile + 2 * HALO
    pieces = ((0, HALO, ap_ref, sp_ref, xp_ref), (HALO, tile, a_ref, s_ref, x_ref),
              (HALO + tile, HALO, an_ref, sn_ref, xn_ref))
    for r0, nr, attn_r, ssm_r, x_r in pieces:
        ssm = jnp.concatenate([ssm_r[k, pl.ds(bi, nr, stride=bsz), :] for k in range(SSM_WIDTH // LANES)], axis=1)
        naext[r0:r0 + nr] = (_rms(attn_r[0]) * ga_ref[...]).astype(BF)
        nsext[r0:r0 + nr] = (_rms(ssm) * gs_ref[...]).astype(BF)
        xext[r0:r0 + nr] = x_r[0]
    mo = _dot(naext[...], wo_ref[0:ATTN_WIDTH, :]) + _dot(nsext[...], wo_ref[ATTN_WIDTH:D_MODEL, :])
    x1 = xext[...] + gt1_ref[0] * mo
    h = _rms(x1) * g2_ref[...]
    h = (h * (1.0 + sc_ref[0]) + sh_ref[0]).astype(BF)
    x = x1[HALO:HALO + tile]
    hext[0:HALO] = jnp.where(t > 0, h[0:HALO], jnp.zeros((HALO, D_MODEL), BF))
    hext[HALO:HALO + tile] = h[HALO:HALO + tile]
    hext[HALO + tile:rows] = jnp.where(t < n_tiles - 1, h[HALO + tile:rows], jnp.zeros((HALO, D_MODEL), BF))
    h_all = hext[...]
    for c in range(D_FF // FF_CHUNK):
        halves = []
        for base in (c * FF_CHUNK, D_FF + c * FF_CHUNK):
            cols = slice(base, base + FF_CHUNK)
            up = _dot(h_all, wup_ref[:, cols])
            prev = pltpu.roll(up, 1, 0)[HALO:HALO + tile]
            nxt = pltpu.roll(up, rows - 1, 0)[HALO:HALO + tile]
            conv = (prev * cw_ref[0:1, cols] + up[HALO:HALO + tile] * cw_ref[1:2, cols]
                    + nxt * cw_ref[2:3, cols] + cb_ref[:, cols])
            halves.append(conv)
        gate, val = halves
        act[:, c * FF_CHUNK:(c + 1) * FF_CHUNK] = (gate * jax.nn.sigmoid(gate) * val).astype(BF)
    y = x + gt_ref[0] * _dot(act[...], wdn_ref[...])
    o_ref[0] = _rms(y) * gf_ref[...]


def _merge_ffn(attn, ssm, x, gate1, g_attn, g_ssm, w_out, shift, scale, gate, g2, w_up, conv_w, conv_b, w_down,
               g_final, tile):
    b, n, _ = x.shape
    n_tiles = n // tile
    per = tile // HALO
    nh = n // HALO
    n_slabs = SSM_WIDTH // LANES
    per_batch = shift.shape[0] > 1
    mod_spec = pl.BlockSpec((1, 1, D_MODEL), (lambda t, i: (i, 0, 0)) if per_batch else (lambda t, i: (0, 0, 0)))
    before = lambda t: jnp.maximum(t * per - 1, 0)
    after = lambda t: jnp.minimum((t + 1) * per, nh - 1)

    def token_specs(width):
        return [pl.BlockSpec((1, tile, width), lambda t, i: (i, t, 0)),
                pl.BlockSpec((1, HALO, width), lambda t, i: (i, before(t), 0)),
                pl.BlockSpec((1, HALO, width), lambda t, i: (i, after(t), 0))]

    slab_specs = [pl.BlockSpec((n_slabs, tile * b, LANES), lambda t, i: (0, t, 0)),
                  pl.BlockSpec((n_slabs, HALO * b, LANES), lambda t, i: (0, before(t), 0)),
                  pl.BlockSpec((n_slabs, HALO * b, LANES), lambda t, i: (0, after(t), 0))]
    rows = tile + 2 * HALO
    return pl.pallas_call(
        functools.partial(_merge_ffn_kernel, tile=tile, n_tiles=n_tiles, bsz=b),
        out_shape=jax.ShapeDtypeStruct((b, n, D_MODEL), F32),
        grid=(n_tiles, b),
        in_specs=token_specs(ATTN_WIDTH) + slab_specs + token_specs(D_MODEL)
        + [mod_spec, _const_spec((1, ATTN_WIDTH)), _const_spec((1, SSM_WIDTH)), _const_spec((D_MODEL, D_MODEL)),
           mod_spec, mod_spec, mod_spec,
           _const_spec((1, D_MODEL)),
           _const_spec((D_MODEL, 2 * D_FF)),
           _const_spec((3, 2 * D_FF)),
           _const_spec((1, 2 * D_FF)),
           _const_spec((D_FF, D_MODEL)),
           _const_spec((1, D_MODEL))],
        out_specs=pl.BlockSpec((1, tile, D_MODEL), lambda t, i: (i, t, 0)),
        scratch_shapes=[pltpu.VMEM((rows, ATTN_WIDTH), BF),
                        pltpu.VMEM((rows, SSM_WIDTH), BF),
                        pltpu.VMEM((rows, D_MODEL), F32),
                        pltpu.VMEM((rows, D_MODEL), BF),
                        pltpu.VMEM((tile, D_FF), BF)],
        compiler_params=_cparams(("arbitrary", "arbitrary")),
        name="merge_conv_ffn",
    )(attn, attn, attn, ssm, ssm, ssm, x, x, x, gate1, g_attn.reshape(1, -1), g_ssm.reshape(1, -1), w_out,
      shift, scale, gate, g2.reshape(1, -1), w_up, conv_w, conv_b.reshape(1, -1), w_down, g_final.reshape(1, -1))


def _dup_heads(w):
    a, b = w[..., :HEAD_DIM], w[..., HEAD_DIM:]
    return jnp.concatenate([a, a, b, b], axis=-1)


def _stream(x, mods, weights, rope_tabs, ctx_kv, h0, ssm_prep, with_kv, tile, ssm_rows, attn_tile):
    sh1, sc1, gt1, sh2, sc2, gt2 = mods
    (g1, g2, w_in, sink, g_attn, g_ssm, w_out, w_up, conv_w, conv_b, w_down, g_final) = weights
    b, n, _ = x.shape
    outs = _in_projection(x, sh1, sc1, g1, w_in, rope_tabs, with_kv, tile)
    q, kk, vv, u = outs[:4]
    if ctx_kv is None:
        attn = _context_attention(q, kk, vv, sink)
    else:
        attn = _latent_attention(q, kk, vv, ctx_kv[0], ctx_kv[1], sink, attn_tile)
    ntap = max(1, SUBLANES // b)
    y_tm, fin_re, fin_im = _s5_scan(u, ssm_prep[ntap], h0[0], h0[1], b, ntap, ssm_rows)
    y = _merge_ffn(attn, y_tm, x, gt1, g_attn, g_ssm, w_out, sh2, sc2, gt2, g2, w_up, conv_w, conv_b, w_down,
                   g_final, tile)
    return y, outs[4:], (fin_re, fin_im)


def kernel(x_prompt, x_sample, cache_k, cache_v, state_ssm_re, state_ssm_im, c, c_ctx, w_ada, b_ada, g_norm1, g_norm2, w_in, attn_sink, ssm_lam_re, ssm_lam_im, ssm_log_dt, ssm_b_re, ssm_b_im, ssm_c_re, ssm_c_im, ssm_d, ssm_w_glu, g_out_attn, g_out_ssm, w_out, w_up, conv_w, conv_b, w_down, g_final):
    depth = w_in.shape[0]
    assert depth == 1
    l = 0
    bp, seq, _ = x_prompt.shape
    bs, dec_seq, _ = x_sample.shape
    past = cache_k.shape[2]

    cond = jnp.concatenate([c_ctx[None, :], c, jnp.zeros((SUBLANES - 1 - bs, D_MODEL), F32)], axis=0)
    mod = _adaln(cond, w_ada[l], b_ada[l])
    mods_ctx = [mod[0:1, i * D_MODEL:(i + 1) * D_MODEL].reshape(1, 1, D_MODEL) for i in range(6)]
    mods_lat = [mod[1:1 + bs, i * D_MODEL:(i + 1) * D_MODEL].reshape(bs, 1, D_MODEL) for i in range(6)]

    wi = w_in[l]
    wq = wi[:, :ATTN_WIDTH]
    wk = wi[:, ATTN_WIDTH:ATTN_WIDTH + KV_WIDTH]
    wv = wi[:, ATTN_WIDTH + KV_WIDTH:ATTN_WIDTH + 2 * KV_WIDTH]
    wu = wi[:, ATTN_WIDTH + 2 * KV_WIDTH:]
    w_in_lat = jnp.concatenate([wq, _dup_heads(wk), _dup_heads(wv), wu], axis=1).astype(BF)
    w_in_ctx = jnp.concatenate([w_in_lat, wk.astype(BF), wv.astype(BF)], axis=1)
    weights = (g_norm1[l], g_norm2[l], None, attn_sink[l], g_out_attn[l], g_out_ssm[l], w_out[l].astype(BF),
               w_up[l].astype(BF), conv_w[l], conv_b[l], w_down[l].astype(BF), g_final)

    ssm_params = (ssm_lam_re[l], ssm_lam_im[l], ssm_log_dt[l], ssm_b_re[l], ssm_b_im[l],
                  ssm_c_re[l], ssm_c_im[l], ssm_d[l], ssm_w_glu[l])
    taps = {max(1, SUBLANES // bp), max(1, SUBLANES // bs)}
    ssm_prep = {nt: _ssm_prepare(*ssm_params, nt) for nt in taps}

    zeros_h0 = jnp.zeros((2, N_GROUP_BLOCKS, bp, STATE_LANES), F32)
    w_ctx = weights[:2] + (w_in_ctx,) + weights[3:]
    y_prompt, (k_new, v_new), (fin_re, fin_im) = _stream(
        x_prompt, mods_ctx, w_ctx, None, None, (zeros_h0, zeros_h0), ssm_prep, True,
        tile=seq, ssm_rows=256, attn_tile=None)
    new_k = k_new.reshape(bp, 1, seq, KV_HEADS, HEAD_DIM)
    new_v = v_new.reshape(bp, 1, seq, KV_HEADS, HEAD_DIM)

    def states_out(fin):
        return fin.transpose(2, 0, 1, 3).reshape(bp, 1, 2, SSM_GROUPS, SSM_STATE)

    ckk = _dup_heads(cache_k[:, l].reshape(bs, past, KV_WIDTH)).astype(BF)
    cvv = _dup_heads(cache_v[:, l].reshape(bs, past, KV_WIDTH)).astype(BF)

    def h0_in(state):
        s = state.astype(F32).reshape(bs, 2, N_GROUP_BLOCKS, STATE_LANES).transpose(1, 2, 0, 3)
        return jnp.concatenate([s] * max(1, SUBLANES // bs), axis=2)

    w_lat = weights[:2] + (w_in_lat,) + weights[3:]
    y_sample, _, _ = _stream(
        x_sample, mods_lat, w_lat, _rope_tables(dec_seq), (ckk, cvv),
        (h0_in(state_ssm_re[:, l]), h0_in(state_ssm_im[:, l])), ssm_prep, False,
        tile=512, ssm_rows=256, attn_tile=512)

    return (y_prompt, y_sample, new_k, new_v, states_out(fin_re), states_out(fin_im))
```

```python
import functools
import math

import jax
import jax.numpy as jnp
from jax import lax
from jax.experimental import pallas as pl
from jax.experimental.pallas import tpu as pltpu

D_MODEL = 1024
HEAD_DIM = 64
N_HEADS = 8
KV_HEADS = 2
ATTN_WIDTH = N_HEADS * HEAD_DIM
KV_WIDTH = KV_HEADS * HEAD_DIM
BLOCK = 128
GRID_W = 64
ROPE_THETA = 10000.0
ROPE_FREQS = HEAD_DIM // 4
SSM_CH = 16
SSM_WIDTH = D_MODEL - ATTN_WIDTH
SSM_GROUPS = SSM_WIDTH // SSM_CH
SSM_STATE = 64
D_FF = 2816
EPS = 1e-6
NEG_INF = -1e30

LANES = 128
SUBLANES = 8
GROUPS_PER_BLOCK = LANES // SSM_CH
N_GROUP_BLOCKS = SSM_GROUPS // GROUPS_PER_BLOCK
STATE_LANES = GROUPS_PER_BLOCK * SSM_STATE
FF_CHUNK = 256
VMEM_LIMIT = 56 * 1024 * 1024

BF = jnp.bfloat16
F32 = jnp.float32


def _cparams(sem):
    return pltpu.CompilerParams(dimension_semantics=sem, vmem_limit_bytes=VMEM_LIMIT)


def _const_spec(shape):
    nd = len(shape)
    return pl.BlockSpec(shape, lambda *_: (0,) * nd, pipeline_mode=pl.Buffered(1))


def _rms(x):
    return x * lax.rsqrt(jnp.mean(x * x, axis=-1, keepdims=True) + EPS)


def _dot(a, b):
    return jnp.dot(a, b, preferred_element_type=F32)


def _dot_nt(a, b):
    return lax.dot_general(a, b, (((1,), (1,)), ((), ())), preferred_element_type=F32)


def _ada_kernel(cond_ref, w_ref, b_ref, o_ref):
    c = cond_ref[...]
    s = c * jax.nn.sigmoid(c)
    o_ref[...] = _dot(s.astype(BF), w_ref[...].astype(BF)) + b_ref[...]


def _adaln(cond, w_ada, b_ada):
    n = cond.shape[0]
    return pl.pallas_call(
        _ada_kernel,
        out_shape=jax.ShapeDtypeStruct((n, 6 * D_MODEL), F32),
        grid=(6,),
        in_specs=[pl.BlockSpec((n, D_MODEL), lambda i: (0, 0)),
                  pl.BlockSpec((D_MODEL, D_MODEL), lambda i: (0, i)),
                  pl.BlockSpec((1, D_MODEL), lambda i: (0, i))],
        out_specs=pl.BlockSpec((n, D_MODEL), lambda i: (0, i)),
        compiler_params=_cparams(("arbitrary",)),
        name="adaln",
    )(cond, w_ada, b_ada.reshape(1, -1))


def _rope(z, cos, sin_lo, sin_hi):
    return z * cos + pltpu.roll(z, LANES - ROPE_FREQS, 1) * sin_lo + pltpu.roll(z, ROPE_FREQS, 1) * sin_hi


def _inproj_kernel(*refs, rope, with_kv, tile, bsz):
    x_ref, sh_ref, sc_ref, g_ref, w_ref = refs[:5]
    pos = 5
    if rope:
        cos_ref, slo_ref, shi_ref = refs[pos:pos + 3]
        pos += 3
    q_ref, kk_ref, vv_ref, u_ref = refs[pos:pos + 4]
    pos += 4
    h = _rms(x_ref[0]) * g_ref[...]
    h = h * (1.0 + sc_ref[0]) + sh_ref[0]
    proj = _dot(h.astype(BF), w_ref[...])
    q = proj[:, :ATTN_WIDTH] * (HEAD_DIM ** -0.5)
    kk = proj[:, ATTN_WIDTH:ATTN_WIDTH + 2 * KV_WIDTH]
    vv = proj[:, ATTN_WIDTH + 2 * KV_WIDTH:ATTN_WIDTH + 4 * KV_WIDTH]
    u0 = ATTN_WIDTH + 4 * KV_WIDTH
    bi = pl.program_id(1)
    for k in range(SSM_WIDTH // LANES):
        u_ref[k, pl.ds(bi, tile, stride=bsz), :] = proj[:, u0 + k * LANES:u0 + (k + 1) * LANES]
    vv_ref[0] = vv.astype(BF)
    if rope:
        cos, slo, shi = cos_ref[...], slo_ref[...], shi_ref[...]
        for c in range(ATTN_WIDTH // LANES):
            q_ref[0, :, c * LANES:(c + 1) * LANES] = _rope(
                q[:, c * LANES:(c + 1) * LANES], cos, slo, shi).astype(BF)
        for c in range(2 * KV_WIDTH // LANES):
            kk_ref[0, :, c * LANES:(c + 1) * LANES] = _rope(
                kk[:, c * LANES:(c + 1) * LANES], cos, slo, shi).astype(BF)
    else:
        q_ref[0] = q.astype(BF)
        kk_ref[0] = kk.astype(BF)
    if with_kv:
        k_ref, v_ref = refs[pos:pos + 2]
        k0 = u0 + SSM_WIDTH
        k_ref[0] = proj[:, k0:k0 + KV_WIDTH]
        v_ref[0] = proj[:, k0 + KV_WIDTH:k0 + 2 * KV_WIDTH]


def _in_projection(x, shift, scale, g, w, rope_tabs, with_kv, tile):
    b, n, _ = x.shape
    nc = w.shape[1]
    per_batch = shift.shape[0] > 1
    mod_spec = pl.BlockSpec((1, 1, D_MODEL), (lambda t, i: (i, 0, 0)) if per_batch else (lambda t, i: (0, 0, 0)))
    in_specs = [pl.BlockSpec((1, tile, D_MODEL), lambda t, i: (i, t, 0)),
                mod_spec, mod_spec,
                _const_spec((1, D_MODEL)),
                _const_spec((D_MODEL, nc))]
    args = [x, shift, scale, g.reshape(1, -1), w]
    if rope_tabs is not None:
        in_specs += [pl.BlockSpec((tile, LANES), lambda t, i: (t, 0))] * 3
        args += list(rope_tabs)

    def out(width, dtype):
        return (jax.ShapeDtypeStruct((b, n, width), dtype),
                pl.BlockSpec((1, tile, width), lambda t, i: (i, t, 0)))

    n_slabs = SSM_WIDTH // LANES
    u_out = (jax.ShapeDtypeStruct((n_slabs, n * b, LANES), F32),
             pl.BlockSpec((n_slabs, tile * b, LANES), lambda t, i: (0, t, 0)))
    outs = [out(ATTN_WIDTH, BF), out(2 * KV_WIDTH, BF), out(2 * KV_WIDTH, BF), u_out]
    if with_kv:
        outs += [out(KV_WIDTH, F32), out(KV_WIDTH, F32)]
    return pl.pallas_call(
        functools.partial(_inproj_kernel, rope=rope_tabs is not None, with_kv=with_kv, tile=tile, bsz=b),
        out_shape=[o[0] for o in outs],
        grid=(n // tile, b),
        in_specs=in_specs,
        out_specs=[o[1] for o in outs],
        compiler_params=_cparams(("arbitrary", "arbitrary")),
        name="in_projection",
    )(*args)


def _rope_tables(n_tokens):
    t = jnp.arange(n_tokens, dtype=F32)
    row = jnp.floor(t / GRID_W)
    col = t - row * GRID_W
    inv_freq = ROPE_THETA ** (-jnp.arange(ROPE_FREQS, dtype=F32) / ROPE_FREQS)
    ang_row = row[:, None] * inv_freq[None, :]
    ang_col = col[:, None] * inv_freq[None, :]
    ang = jnp.concatenate([ang_row, ang_row, ang_col, ang_col] * (LANES // HEAD_DIM), axis=-1)
    first = (jnp.arange(LANES) % (2 * ROPE_FREQS)) < ROPE_FREQS
    sin = jnp.sin(ang)
    return jnp.cos(ang), jnp.where(first, -sin, 0.0), jnp.where(first, 0.0, sin)


def _stack_heads(q_rows, j, lo):
    qa = q_rows[:, (2 * j) * LANES:(2 * j + 1) * LANES]
    qb = q_rows[:, (2 * j + 1) * LANES:(2 * j + 2) * LANES]
    keep_lo = jnp.where(lo, 1.0, 0.0).astype(BF)
    keep_hi = jnp.where(lo, 0.0, 1.0).astype(BF)
    return jnp.concatenate([qa * keep_lo, qa * keep_hi, qb * keep_lo, qb * keep_hi], axis=0)


def _sink_column(sink_ref, j, t):
    return jnp.concatenate([jnp.full((t, 1), sink_ref[4 * j + a], F32) for a in range(4)], axis=0)


def _softmax_pv(scores, values, sink_col):
    m = sink_col
    for s in scores:
        m = jnp.maximum(m, jnp.max(s, axis=-1, keepdims=True))
    acc = None
    for s, v in zip(scores, values):
        o = _dot(jnp.exp(s - m).astype(BF), v)
        acc = o if acc is None else acc + o
    den = acc[:, LANES:2 * LANES] + jnp.exp(sink_col - m)
    return acc[:, 0:LANES] / den


def _unstack_store(o_ref, o, rows, j, lo, t):
    o_ref[0, rows, (2 * j) * LANES:(2 * j + 1) * LANES] = jnp.where(lo, o[0:t], o[t:2 * t])
    o_ref[0, rows, (2 * j + 1) * LANES:(2 * j + 2) * LANES] = jnp.where(lo, o[2 * t:3 * t], o[3 * t:4 * t])


def _fill_values(vext, r0, vv_ref):
    nr = vv_ref.shape[1]
    for j in range(KV_HEADS):
        vext[r0:r0 + nr, 2 * j * LANES:(2 * j + 1) * LANES] = vv_ref[0, :, j * LANES:(j + 1) * LANES]
        vext[r0:r0 + nr, (2 * j + 1) * LANES:(2 * j + 2) * LANES] = jnp.ones((nr, LANES), BF)


def _ctx_attn_kernel(sink_ref, q_ref, kk_ref, vv_ref, o_ref, vext, *, n):
    _fill_values(vext, 0, vv_ref)
    lo = lax.broadcasted_iota(jnp.int32, (BLOCK, LANES), 1) < HEAD_DIM
    for sub in range(n // BLOCK):
        rows = slice(sub * BLOCK, (sub + 1) * BLOCK)
        q_rows = q_ref[0, rows, :]
        for j in range(KV_HEADS):
            cols = slice(j * LANES, (j + 1) * LANES)
            qs = _stack_heads(q_rows, j, lo)
            s = _dot_nt(qs, kk_ref[0, :, cols])
            o = _softmax_pv([s], [vext[:, 2 * j * LANES:(2 * j + 2) * LANES]], _sink_column(sink_ref, j, BLOCK))
            _unstack_store(o_ref, o, rows, j, lo, BLOCK)


def _context_attention(q, kk, vv, sink):
    b, n, _ = q.shape
    return pl.pallas_call(
        functools.partial(_ctx_attn_kernel, n=n),
        out_shape=jax.ShapeDtypeStruct((b, n, ATTN_WIDTH), F32),
        grid=(b,),
        in_specs=[pl.BlockSpec(memory_space=pltpu.SMEM),
                  pl.BlockSpec((1, n, ATTN_WIDTH), lambda i: (i, 0, 0)),
                  pl.BlockSpec((1, n, 2 * KV_WIDTH), lambda i: (i, 0, 0)),
                  pl.BlockSpec((1, n, 2 * KV_WIDTH), lambda i: (i, 0, 0))],
        out_specs=pl.BlockSpec((1, n, ATTN_WIDTH), lambda i: (i, 0, 0)),
        scratch_shapes=[pltpu.VMEM((n, 4 * KV_WIDTH), BF)],
        compiler_params=_cparams(("arbitrary",)),
        name="context_attention",
    )(sink, q, kk, vv)


def _lat_attn_kernel(sink_ref, q_ref, kkm_ref, kkp_ref, kkn_ref, vvm_ref, vvp_ref, vvn_ref,
                     ckk_ref, cvv_ref, o_ref, kext, vext, cvext, *, tq, n):
    t = pl.program_id(1)
    kext[0:BLOCK] = kkp_ref[0]
    kext[BLOCK:BLOCK + tq] = kkm_ref[0]
    kext[BLOCK + tq:2 * BLOCK + tq] = kkn_ref[0]
    _fill_values(vext, 0, vvp_ref)
    _fill_values(vext, BLOCK, vvm_ref)
    _fill_values(vext, BLOCK + tq, vvn_ref)
    _fill_values(cvext, 0, cvv_ref)
    lo = lax.broadcasted_iota(jnp.int32, (BLOCK, LANES), 1) < HEAD_DIM
    r = lax.broadcasted_iota(jnp.int32, (BLOCK, 3 * BLOCK), 0)
    w = lax.broadcasted_iota(jnp.int32, (BLOCK, 3 * BLOCK), 1)
    band = (w >= r) & (w <= r + 2 * BLOCK)
    for sub in range(tq // BLOCK):
        rows = slice(sub * BLOCK, (sub + 1) * BLOCK)
        key = t * tq + (sub - 1) * BLOCK + w
        valid = jnp.where(band & (key >= 0) & (key < n), 1.0, 0.0)
        valid4 = jnp.concatenate([valid] * 4, axis=0) > 0.5
        q_rows = q_ref[0, rows, :]
        win = slice(sub * BLOCK, (sub + 3) * BLOCK)
        for j in range(KV_HEADS):
            cols = slice(j * LANES, (j + 1) * LANES)
            qs = _stack_heads(q_rows, j, lo)
            s_loc = jnp.where(valid4, _dot_nt(qs, kext[win, cols]), NEG_INF)
            s_ctx = _dot_nt(qs, ckk_ref[0, :, cols])
            vcols = slice(2 * j * LANES, (2 * j + 2) * LANES)
            o = _softmax_pv([s_loc, s_ctx], [vext[win, vcols], cvext[:, vcols]],
                            _sink_column(sink_ref, j, BLOCK))
            _unstack_store(o_ref, o, rows, j, lo, BLOCK)


def _latent_attention(q, kk, vv, ckk, cvv, sink, tq):
    b, n, _ = q.shape
    past = ckk.shape[1]
    nblk = n // BLOCK
    per = tq // BLOCK
    kw = 2 * KV_WIDTH
    main = pl.BlockSpec((1, tq, kw), lambda i, t: (i, t, 0))
    prev = pl.BlockSpec((1, BLOCK, kw), lambda i, t: (i, jnp.maximum(t * per - 1, 0), 0))
    nxt = pl.BlockSpec((1, BLOCK, kw), lambda i, t: (i, jnp.minimum((t + 1) * per, nblk - 1), 0))
    ctx = pl.BlockSpec((1, past, kw), lambda i, t: (i, 0, 0))
    return pl.pallas_call(
        functools.partial(_lat_attn_kernel, tq=tq, n=n),
        out_shape=jax.ShapeDtypeStruct((b, n, ATTN_WIDTH), F32),
        grid=(b, n // tq),
        in_specs=[pl.BlockSpec(memory_space=pltpu.SMEM),
                  pl.BlockSpec((1, tq, ATTN_WIDTH), lambda i, t: (i, t, 0)),
                  main, prev, nxt, main, prev, nxt, ctx, ctx],
        out_specs=pl.BlockSpec((1, tq, ATTN_WIDTH), lambda i, t: (i, t, 0)),
        scratch_shapes=[pltpu.VMEM((tq + 2 * BLOCK, kw), BF), pltpu.VMEM((tq + 2 * BLOCK, 2 * kw), BF),
                        pltpu.VMEM((past, 2 * kw), BF)],
        compiler_params=_cparams(("arbitrary", "arbitrary")),
        name="latent_attention",
    )(sink, q, kk, kk, kk, vv, vv, vv, ckk, cvv)


def _ssm_prepare(lam_re, lam_im, log_dt, b_re, b_im, c_re, c_im, d, w_glu, ntap):
    lam_re, lam_im = lam_re.astype(F32), lam_im.astype(F32)
    dt = jnp.exp(log_dt.astype(F32))[..., None]
    mag = jnp.exp(lam_re * dt)
    ang = lam_im * dt
    a_re, a_im = mag * jnp.cos(ang), mag * jnp.sin(ang)
    den = lam_re * lam_re + lam_im * lam_im
    num_re = a_re - 1.0
    coef_re = (num_re * lam_re + a_im * lam_im) / den
    coef_im = (a_im * lam_re - num_re * lam_im) / den
    br, bi = b_re.astype(F32), b_im.astype(F32)
    t_re = coef_re[..., None] * br - coef_im[..., None] * bi
    t_im = coef_re[..., None] * bi + coef_im[..., None] * br
    eye = jnp.eye(GROUPS_PER_BLOCK, dtype=F32)

    def block_diag_in(t):
        t = t.reshape(2, N_GROUP_BLOCKS, GROUPS_PER_BLOCK, SSM_STATE, SSM_CH)
        return jnp.einsum("dbgsp,gh->dbgphs", t, eye).reshape(2, N_GROUP_BLOCKS, LANES, STATE_LANES)

    taps = []
    p_re, p_im = jnp.ones_like(a_re), jnp.zeros_like(a_re)
    for _ in range(ntap):
        k_re = p_re[..., None] * t_re - p_im[..., None] * t_im
        k_im = p_re[..., None] * t_im + p_im[..., None] * t_re
        taps.append(jnp.concatenate([block_diag_in(k_re), block_diag_in(k_im)], axis=-1))
        p_re, p_im = p_re * a_re - p_im * a_im, p_re * a_im + p_im * a_re
    w_b = jnp.concatenate(taps, axis=2).astype(BF)

    def block_diag_out(c):
        c = c.astype(F32).reshape(2, N_GROUP_BLOCKS, GROUPS_PER_BLOCK, SSM_CH, SSM_STATE)
        return jnp.einsum("dbgps,gh->dbgshp", c, eye).reshape(2, N_GROUP_BLOCKS, STATE_LANES, LANES)

    w_c = jnp.concatenate([block_diag_out(c_re), -block_diag_out(c_im)], axis=2).astype(BF)

    def lanes(z):
        return z.reshape(2, N_GROUP_BLOCKS, 1, STATE_LANES)

    coef = jnp.concatenate([lanes(p_re), lanes(p_im), lanes(a_re), lanes(a_im)], axis=2)
    wg = w_glu.astype(F32).reshape(N_GROUP_BLOCKS, GROUPS_PER_BLOCK, SSM_CH, SSM_CH)
    w_g = jnp.einsum("bgpq,gh->bgphq", wg, eye).reshape(N_GROUP_BLOCKS, LANES, LANES).astype(BF)
    return w_b, w_c, coef, d.astype(F32).reshape(1, SSM_WIDTH), w_g


def _ssm_kernel(u_ref, wb_ref, wc_ref, coef_ref, h0re_ref, h0im_ref, d_ref, wg_ref,
                y_ref, fre_ref, fim_ref, ybwd, ubuf, vstate, *bufs, total, chunk, step_rows, bsz, ntap):
    bu = (bufs[0:2], bufs[2:4])
    hbf = (bufs[4:6], bufs[6:8])
    n_chunks = total // chunk
    n_steps = chunk // step_rows
    pack_steps = 2 * SUBLANES // step_rows if step_rows < 2 * SUBLANES else 1
    row = lax.broadcasted_iota(jnp.int32, (step_rows, STATE_LANES), 0)

    def project(d, c, slot):
        r0 = pl.multiple_of(c * chunk, chunk)
        u = u_ref[0, pl.ds(r0, chunk), :]
        if ntap > 1:
            if d == 0:
                ubuf[d, SUBLANES:SUBLANES + chunk] = u
            else:
                ubuf[d, 0:chunk] = u
            shifted = ubuf[d, bsz:bsz + chunk]
            if d == 0:
                ubuf[d, 0:SUBLANES] = u[chunk - SUBLANES:chunk]
            else:
                ubuf[d, chunk:chunk + SUBLANES] = u[0:SUBLANES]
            lhs = jnp.concatenate([u, shifted], axis=1)
        else:
            lhs = u
        bu[slot][d][...] = _dot(lhs.astype(BF), wb_ref[d, 0])

    def scan(slot, first):
        ar, ai, v_r, v_i, e_re, e_im = [], [], [], [], [], []
        for d in range(2):
            coef = coef_ref[d, 0]
            ar.append(jnp.broadcast_to(coef[0:1], (step_rows, STATE_LANES)))
            ai.append(jnp.broadcast_to(coef[1:2], (step_rows, STATE_LANES)))
            v_r.append(vstate[d, 0])
            v_i.append(vstate[d, 1])
            if ntap > 1:
                h0r, h0i = h0re_ref[d, 0], h0im_ref[d, 0]
                a_re, a_im = coef[2:3], coef[3:4]
                older = (row < bsz) if d == 0 else (row >= bsz)
                e_re.append(jnp.where(older, a_re * h0r - a_im * h0i, 0.0) * first)
                e_im.append(jnp.where(older, a_re * h0i + a_im * h0r, 0.0) * first)
        held = [[], []]
        for k in range(n_steps):
            for d in range(2):
                s = k if d == 0 else n_steps - 1 - k
                rows = slice(s * step_rows, (s + 1) * step_rows)
                b_r = bu[slot][d][rows, 0:STATE_LANES]
                b_i = bu[slot][d][rows, STATE_LANES:2 * STATE_LANES]
                if ntap > 1 and k == 0:
                    b_r, b_i = b_r + e_re[d], b_i + e_im[d]
                v_r[d], v_i[d] = ar[d] * v_r[d] - ai[d] * v_i[d] + b_r, ar[d] * v_i[d] + ai[d] * v_r[d] + b_i
                held[d].append((s, v_r[d], v_i[d]))
                if len(held[d]) == pack_steps:
                    hs = sorted(held[d], key=lambda h: h[0])
                    r16 = slice(hs[0][0] * step_rows, (hs[-1][0] + 1) * step_rows)
                    hbf[slot][d][r16, 0:STATE_LANES] = jnp.concatenate([h[1] for h in hs], axis=0).astype(BF)
                    hbf[slot][d][r16, STATE_LANES:2 * STATE_LANES] = jnp.concatenate(
                        [h[2] for h in hs], axis=0).astype(BF)
                    held[d] = []
        for d in range(2):
            vstate[d, 0] = v_r[d]
            vstate[d, 1] = v_i[d]

    def readout(slot, c):
        for d in range(2):
            y_part = _dot(hbf[slot][d][...], wc_ref[d, 0])
            cc = c if d == 0 else n_chunks - 1 - c
            r0 = pl.multiple_of(cc * chunk, chunk)
            if d == 0:
                y_ref[0, pl.ds(r0, chunk), :] = y_part
            else:
                ybwd[pl.ds(r0, chunk), :] = y_part

    for d in range(2):
        h0r, h0i = h0re_ref[d, 0], h0im_ref[d, 0]
        if ntap > 1:
            newer = (row >= bsz) if d == 0 else (row < bsz)
            vstate[d, 0] = jnp.where(newer, h0r, 0.0)
            vstate[d, 1] = jnp.where(newer, h0i, 0.0)
        else:
            vstate[d, 0] = h0r
            vstate[d, 1] = h0i
    if ntap > 1:
        ubuf[...] = jnp.zeros_like(ubuf)
    for d in range(2):
        hbf[1][d][...] = jnp.zeros((chunk, 2 * STATE_LANES), BF)
    project(0, 0, 0)
    project(1, n_chunks - 1, 0)

    def body(pair, carry):
        for slot in range(2):
            c = 2 * pair + slot
            first = jnp.where(c == 0, 1.0, 0.0)
            project(0, jnp.minimum(c + 1, n_chunks - 1), 1 - slot)
            project(1, jnp.maximum(n_chunks - 2 - c, 0), 1 - slot)
            readout(1 - slot, jnp.maximum(c - 1, 0))
            scan(slot, first)
        return carry

    lax.fori_loop(0, n_chunks // 2, body, 0)
    readout(1, n_chunks - 1)
    for d in range(2):
        fre_ref[d, 0] = vstate[d, 0]
        fim_ref[d, 0] = vstate[d, 1]

    fin_rows = 4 * chunk

    def finish(c, carry):
        r0 = pl.multiple_of(c * fin_rows, fin_rows)
        rows = pl.ds(r0, fin_rows)
        y = y_ref[0, rows, :] + ybwd[rows, :] + u_ref[0, rows, :] * d_ref[...]
        x = jax.nn.gelu(y, approximate=True)
        y_ref[0, rows, :] = x * jax.nn.sigmoid(_dot(x.astype(BF), wg_ref[0]))
        return carry

    lax.fori_loop(0, total // fin_rows, finish, 0)


def _s5_scan(u_tm, prep, h0_re, h0_im, bsz, ntap, rows):
    w_b, w_c, coef, d_vec, w_g = prep
    total = u_tm.shape[1]
    chunk = rows
    step_rows = bsz * ntap
    kb = LANES * ntap

    dir_spec = lambda shape: pl.BlockSpec((2, 1) + shape, lambda g: (0, g, 0, 0))
    state_shape = jax.ShapeDtypeStruct((2, N_GROUP_BLOCKS, step_rows, STATE_LANES), F32)
    return pl.pallas_call(
        functools.partial(_ssm_kernel, total=total, chunk=chunk, step_rows=step_rows, bsz=bsz, ntap=ntap),
        out_shape=[jax.ShapeDtypeStruct((N_GROUP_BLOCKS, total, LANES), F32), state_shape, state_shape],
        grid=(N_GROUP_BLOCKS,),
        in_specs=[pl.BlockSpec((1, total, LANES), lambda g: (g, 0, 0)),
                  dir_spec((kb, 2 * STATE_LANES)),
                  dir_spec((2 * STATE_LANES, LANES)),
                  dir_spec((4, STATE_LANES)),
                  dir_spec((step_rows, STATE_LANES)),
                  dir_spec((step_rows, STATE_LANES)),
                  pl.BlockSpec((1, LANES), lambda g: (0, g)),
                  pl.BlockSpec((1, LANES, LANES), lambda g: (g, 0, 0))],
        out_specs=[pl.BlockSpec((1, total, LANES), lambda g: (g, 0, 0)),
                   dir_spec((step_rows, STATE_LANES)),
                   dir_spec((step_rows, STATE_LANES))],
        scratch_shapes=[pltpu.VMEM((total, LANES), F32),
                        pltpu.VMEM((2, chunk + SUBLANES, LANES), F32),
                        pltpu.VMEM((2, 2, step_rows, STATE_LANES), F32)]
        + [pltpu.VMEM((chunk, 2 * STATE_LANES), F32)] * 4
        + [pltpu.VMEM((chunk, 2 * STATE_LANES), BF)] * 4,
        compiler_params=_cparams(("arbitrary",)),
        name="s5_scan",
    )(u_tm, w_b, w_c, coef, h0_re, h0_im, d_vec, w_g)


HALO = 16


def _merge_ffn_kernel(a_ref, ap_ref, an_ref, s_ref, sp_ref, sn_ref, x_ref, xp_ref, xn_ref,
                      gt1_ref, ga_ref, gs_ref, wo_ref,
                      sh_ref, sc_ref, gt_ref, g2_ref, wup_ref, cw_ref, cb_ref, wdn_ref, gf_ref,
                      o_ref, naext, nsext, xext, hext, act, *, tile, n_tiles, bsz):
    t = pl.program_id(0)
    bi = pl.program_id(1)
    rows = tile + 2 * HALO
    pieces = ((0, HALO, ap_ref, sp_ref, xp_ref), (HALO, tile, a_ref, s_ref, x_ref),
              (HALO + tile, HALO, an_ref, sn_ref, xn_ref))
    for r0, nr, attn_r, ssm_r, x_r in pieces:
        ssm = jnp.concatenate([ssm_r[k, pl.ds(bi, nr, stride=bsz), :] for k in range(SSM_WIDTH // LANES)], axis=1)
        naext[r0:r0 + nr] = (_rms(attn_r[0]) * ga_ref[...]).astype(BF)
        nsext[r0:r0 + nr] = (_rms(ssm) * gs_ref[...]).astype(BF)
        xext[r0:r0 + nr] = x_r[0]
    mo = _dot(naext[...], wo_ref[0:ATTN_WIDTH, :]) + _dot(nsext[...], wo_ref[ATTN_WIDTH:D_MODEL, :])
    x1 = xext[...] + gt1_ref[0] * mo
    h = _rms(x1) * g2_ref[...]
    h = (h * (1.0 + sc_ref[0]) + sh_ref[0]).astype(BF)
    x = x1[HALO:HALO + tile]
    hext[0:HALO] = jnp.where(t > 0, h[0:HALO], jnp.zeros((HALO, D_MODEL), BF))
    hext[HALO:HALO + tile] = h[HALO:HALO + tile]
    hext[HALO + tile:rows] = jnp.where(t < n_tiles - 1, h[HALO + tile:rows], jnp.zeros((HALO, D_MODEL), BF))
    h_all = hext[...]
    for c in range(D_FF // FF_CHUNK):
        halves = []
        for base in (c * FF_CHUNK, D_FF + c * FF_CHUNK):
            cols = slice(base, base + FF_CHUNK)
            up = _dot(h_all, wup_ref[:, cols])
            prev = pltpu.roll(up, 1, 0)[HALO:HALO + tile]
            nxt = pltpu.roll(up, rows - 1, 0)[HALO:HALO + tile]
            conv = (prev * cw_ref[0:1, cols] + up[HALO:HALO + tile] * cw_ref[1:2, cols]
                    + nxt * cw_ref[2:3, cols] + cb_ref[:, cols])
            halves.append(conv)
        gate, val = halves
        act[:, c * FF_CHUNK:(c + 1) * FF_CHUNK] = (gate * jax.nn.sigmoid(gate) * val).astype(BF)
    y = x + gt_ref[0] * _dot(act[...], wdn_ref[...])
    o_ref[0] = _rms(y) * gf_ref[...]


def _merge_ffn(attn, ssm, x, gate1, g_attn, g_ssm, w_out, shift, scale, gate, g2, w_up, conv_w, conv_b, w_down,
               g_final, tile):
    b, n, _ = x.shape
    n_tiles = n // tile
    per = tile // HALO
    nh = n // HALO
    n_slabs = SSM_WIDTH // LANES
    per_batch = shift.shape[0] > 1
    mod_spec = pl.BlockSpec((1, 1, D_MODEL), (lambda t, i: (i, 0, 0)) if per_batch else (lambda t, i: (0, 0, 0)))
    before = lambda t: jnp.maximum(t * per - 1, 0)
    after = lambda t: jnp.minimum((t + 1) * per, nh - 1)

    def token_specs(width):
        return [pl.BlockSpec((1, tile, width), lambda t, i: (i, t, 0)),
                pl.BlockSpec((1, HALO, width), lambda t, i: (i, before(t), 0)),
                pl.BlockSpec((1, HALO, width), lambda t, i: (i, after(t), 0))]

    slab_specs = [pl.BlockSpec((n_slabs, tile * b, LANES), lambda t, i: (0, t, 0)),
                  pl.BlockSpec((n_slabs, HALO * b, LANES), lambda t, i: (0, before(t), 0)),
                  pl.BlockSpec((n_slabs, HALO * b, LANES), lambda t, i: (0, after(t), 0))]
    rows = tile + 2 * HALO
    return pl.pallas_call(
        functools.partial(_merge_ffn_kernel, tile=tile, n_tiles=n_tiles, bsz=b),
        out_shape=jax.ShapeDtypeStruct((b, n, D_MODEL), F32),
        grid=(n_tiles, b),
        in_specs=token_specs(ATTN_WIDTH) + slab_specs + token_specs(D_MODEL)
        + [mod_spec, _const_spec((1, ATTN_WIDTH)), _const_spec((1, SSM_WIDTH)), _const_spec((D_MODEL, D_MODEL)),
           mod_spec, mod_spec, mod_spec,
           _const_spec((1, D_MODEL)),
           _const_spec((D_MODEL, 2 * D_FF)),
           _const_spec((3, 2 * D_FF)),
           _const_spec((1, 2 * D_FF)),
           _const_spec((D_FF, D_MODEL)),
           _const_spec((1, D_MODEL))],
        out_specs=pl.BlockSpec((1, tile, D_MODEL), lambda t, i: (i, t, 0)),
        scratch_shapes=[pltpu.VMEM((rows, ATTN_WIDTH), BF),
                        pltpu.VMEM((rows, SSM_WIDTH), BF),
                        pltpu.VMEM((rows, D_MODEL), F32),
                        pltpu.VMEM((rows, D_MODEL), BF),
                        pltpu.VMEM((tile, D_FF), BF)],
        compiler_params=_cparams(("arbitrary", "arbitrary")),
        name="merge_conv_ffn",
    )(attn, attn, attn, ssm, ssm, ssm, x, x, x, gate1, g_attn.reshape(1, -1), g_ssm.reshape(1, -1), w_out,
      shift, scale, gate, g2.reshape(1, -1), w_up, conv_w, conv_b.reshape(1, -1), w_down, g_final.reshape(1, -1))


def _dup_heads(w):
    a, b = w[..., :HEAD_DIM], w[..., HEAD_DIM:]
    return jnp.concatenate([a, a, b, b], axis=-1)


def _stream(x, mods, weights, rope_tabs, ctx_kv, h0, ssm_prep, with_kv, tile, ssm_rows, attn_tile):
    sh1, sc1, gt1, sh2, sc2, gt2 = mods
    (g1, g2, w_in, sink, g_attn, g_ssm, w_out, w_up, conv_w, conv_b, w_down, g_final) = weights
    b, n, _ = x.shape
    outs = _in_projection(x, sh1, sc1, g1, w_in, rope_tabs, with_kv, tile)
    q, kk, vv, u = outs[:4]
    if ctx_kv is None:
        attn = _context_attention(q, kk, vv, sink)
    else:
        attn = _latent_attention(q, kk, vv, ctx_kv[0], ctx_kv[1], sink, attn_tile)
    ntap = max(1, SUBLANES // b)
    y_tm, fin_re, fin_im = _s5_scan(u, ssm_prep[ntap], h0[0], h0[1], b, ntap, ssm_rows)
    y = _merge_ffn(attn, y_tm, x, gt1, g_attn, g_ssm, w_out, sh2, sc2, gt2, g2, w_up, conv_w, conv_b, w_down,
                   g_final, tile)
    return y, outs[4:], (fin_re, fin_im)


def kernel(x_prompt, x_sample, cache_k, cache_v, state_ssm_re, state_ssm_im, c, c_ctx, w_ada, b_ada, g_norm1, g_norm2, w_in, attn_sink, ssm_lam_re, ssm_lam_im, ssm_log_dt, ssm_b_re, ssm_b_im, ssm_c_re, ssm_c_im, ssm_d, ssm_w_glu, g_out_attn, g_out_ssm, w_out, w_up, conv_w, conv_b, w_down, g_final):
    depth = w_in.shape[0]
    assert depth == 1
    l = 0
    bp, seq, _ = x_prompt.shape
    bs, dec_seq, _ = x_sample.shape
    past = cache_k.shape[2]

    cond = jnp.concatenate([c_ctx[None, :], c, jnp.zeros((SUBLANES - 1 - bs, D_MODEL), F32)], axis=0)
    mod = _adaln(cond, w_ada[l], b_ada[l])
    mods_ctx = [mod[0:1, i * D_MODEL:(i + 1) * D_MODEL].reshape(1, 1, D_MODEL) for i in range(6)]
    mods_lat = [mod[1:1 + bs, i * D_MODEL:(i + 1) * D_MODEL].reshape(bs, 1, D_MODEL) for i in range(6)]

    wi = w_in[l]
    wq = wi[:, :ATTN_WIDTH]
    wk = wi[:, ATTN_WIDTH:ATTN_WIDTH + KV_WIDTH]
    wv = wi[:, ATTN_WIDTH + KV_WIDTH:ATTN_WIDTH + 2 * KV_WIDTH]
    wu = wi[:, ATTN_WIDTH + 2 * KV_WIDTH:]
    w_in_lat = jnp.concatenate([wq, _dup_heads(wk), _dup_heads(wv), wu], axis=1).astype(BF)
    w_in_ctx = jnp.concatenate([w_in_lat, wk.astype(BF), wv.astype(BF)], axis=1)
    weights = (g_norm1[l], g_norm2[l], None, attn_sink[l], g_out_attn[l], g_out_ssm[l], w_out[l].astype(BF),
               w_up[l].astype(BF), conv_w[l], conv_b[l], w_down[l].astype(BF), g_final)

    ssm_params = (ssm_lam_re[l], ssm_lam_im[l], ssm_log_dt[l], ssm_b_re[l], ssm_b_im[l],
                  ssm_c_re[l], ssm_c_im[l], ssm_d[l], ssm_w_glu[l])
    taps = {max(1, SUBLANES // bp), max(1, SUBLANES // bs)}
    ssm_prep = {nt: _ssm_prepare(*ssm_params, nt) for nt in taps}

    zeros_h0 = jnp.zeros((2, N_GROUP_BLOCKS, bp, STATE_LANES), F32)
    w_ctx = weights[:2] + (w_in_ctx,) + weights[3:]
    y_prompt, (k_new, v_new), (fin_re, fin_im) = _stream(
        x_prompt, mods_ctx, w_ctx, None, None, (zeros_h0, zeros_h0), ssm_prep, True,
        tile=seq, ssm_rows=256, attn_tile=None)
    new_k = k_new.reshape(bp, 1, seq, KV_HEADS, HEAD_DIM)
    new_v = v_new.reshape(bp, 1, seq, KV_HEADS, HEAD_DIM)

    def states_out(fin):
        return fin.transpose(2, 0, 1, 3).reshape(bp, 1, 2, SSM_GROUPS, SSM_STATE)

    ckk = _dup_heads(cache_k[:, l].reshape(bs, past, KV_WIDTH)).astype(BF)
    cvv = _dup_heads(cache_v[:, l].reshape(bs, past, KV_WIDTH)).astype(BF)

    def h0_in(state):
        s = state.astype(F32).reshape(bs, 2, N_GROUP_BLOCKS, STATE_LANES).transpose(1, 2, 0, 3)
        return jnp.concatenate([s] * max(1, SUBLANES // bs), axis=2)

    w_lat = weights[:2] + (w_in_lat,) + weights[3:]
    y_sample, _, _ = _stream(
        x_sample, mods_lat, w_lat, _rope_tables(dec_seq), (ckk, cvv),
        (h0_in(state_ssm_re[:, l]), h0_in(state_ssm_im[:, l])), ssm_prep, False,
        tile=512, ssm_rows=256, attn_tile=512)

    return (y_prompt, y_sample, new_k, new_v, states_out(fin_re), states_out(fin_im))
```

```python
import functools
import math

import jax
import jax.numpy as jnp
from jax import lax
from jax.experimental import pallas as pl
from jax.experimental.pallas import tpu as pltpu

D_MODEL = 1024
HEAD_DIM = 64
N_HEADS = 8
KV_HEADS = 2
ATTN_WIDTH = N_HEADS * HEAD_DIM
KV_WIDTH = KV_HEADS * HEAD_DIM
BLOCK = 128
GRID_W = 64
ROPE_THETA = 10000.0
ROPE_FREQS = HEAD_DIM // 4
SSM_CH = 16
SSM_WIDTH = D_MODEL - ATTN_WIDTH
SSM_GROUPS = SSM_WIDTH // SSM_CH
SSM_STATE = 64
D_FF = 2816
EPS = 1e-6
NEG_INF = -1e30
LOG2E = math.log2(math.e)

LANES = 128
SUBLANES = 8
GROUPS_PER_BLOCK = LANES // SSM_CH
N_GROUP_BLOCKS = SSM_GROUPS // GROUPS_PER_BLOCK
STATE_LANES = GROUPS_PER_BLOCK * SSM_STATE
FF_CHUNK = 256
VMEM_LIMIT = 56 * 1024 * 1024

BF = jnp.bfloat16
F32 = jnp.float32


def _cparams(sem):
    return pltpu.CompilerParams(dimension_semantics=sem, vmem_limit_bytes=VMEM_LIMIT)


def _const_spec(shape):
    nd = len(shape)
    return pl.BlockSpec(shape, lambda *_: (0,) * nd, pipeline_mode=pl.Buffered(1))


def _rms(x):
    return x * lax.rsqrt(jnp.mean(x * x, axis=-1, keepdims=True) + EPS)


def _dot(a, b):
    return jnp.dot(a, b, preferred_element_type=F32)


def _dot_nt(a, b):
    return lax.dot_general(a, b, (((1,), (1,)), ((), ())), preferred_element_type=F32)


def _ada_kernel(cond_ref, w_ref, b_ref, o_ref):
    c = cond_ref[...]
    s = c * jax.nn.sigmoid(c)
    o_ref[...] = _dot(s.astype(BF), w_ref[...].astype(BF)) + b_ref[...]


def _adaln(cond, w_ada, b_ada):
    n = cond.shape[0]
    return pl.pallas_call(
        _ada_kernel,
        out_shape=jax.ShapeDtypeStruct((n, 6 * D_MODEL), F32),
        grid=(6,),
        in_specs=[pl.BlockSpec((n, D_MODEL), lambda i: (0, 0)),
                  pl.BlockSpec((D_MODEL, D_MODEL), lambda i: (0, i)),
                  pl.BlockSpec((1, D_MODEL), lambda i: (0, i))],
        out_specs=pl.BlockSpec((n, D_MODEL), lambda i: (0, i)),
        compiler_params=_cparams(("arbitrary",)),
        name="adaln",
    )(cond, w_ada, b_ada.reshape(1, -1))


def _rope(z, cos, sin_lo, sin_hi):
    return z * cos + pltpu.roll(z, LANES - ROPE_FREQS, 1) * sin_lo + pltpu.roll(z, ROPE_FREQS, 1) * sin_hi


def _inproj_kernel(*refs, rope, with_kv, tile, bsz):
    x_ref, sh_ref, sc_ref, g_ref, w_ref = refs[:5]
    pos = 5
    if rope:
        cos_ref, slo_ref, shi_ref = refs[pos:pos + 3]
        pos += 3
    q_ref, kk_ref, vv_ref, u_ref = refs[pos:pos + 4]
    pos += 4
    h = _rms(x_ref[0]) * g_ref[...]
    h = h * (1.0 + sc_ref[0]) + sh_ref[0]
    proj = _dot(h.astype(BF), w_ref[...])
    q = proj[:, :ATTN_WIDTH] * (HEAD_DIM ** -0.5 * LOG2E)
    kk = proj[:, ATTN_WIDTH:ATTN_WIDTH + 2 * KV_WIDTH]
    vv = proj[:, ATTN_WIDTH + 2 * KV_WIDTH:ATTN_WIDTH + 4 * KV_WIDTH]
    u0 = ATTN_WIDTH + 4 * KV_WIDTH
    bi = pl.program_id(1)
    for k in range(SSM_WIDTH // LANES):
        u_ref[k, pl.ds(bi, tile, stride=bsz), :] = proj[:, u0 + k * LANES:u0 + (k + 1) * LANES]
    vv_ref[0] = vv.astype(BF)
    if rope:
        cos, slo, shi = cos_ref[...], slo_ref[...], shi_ref[...]
        for c in range(ATTN_WIDTH // LANES):
            q_ref[0, :, c * LANES:(c + 1) * LANES] = _rope(
                q[:, c * LANES:(c + 1) * LANES], cos, slo, shi).astype(BF)
        for c in range(2 * KV_WIDTH // LANES):
            kk_ref[0, :, c * LANES:(c + 1) * LANES] = _rope(
                kk[:, c * LANES:(c + 1) * LANES], cos, slo, shi).astype(BF)
    else:
        q_ref[0] = q.astype(BF)
        kk_ref[0] = kk.astype(BF)
    if with_kv:
        k_ref, v_ref = refs[pos:pos + 2]
        k0 = u0 + SSM_WIDTH
        k_ref[0] = proj[:, k0:k0 + KV_WIDTH]
        v_ref[0] = proj[:, k0 + KV_WIDTH:k0 + 2 * KV_WIDTH]


def _in_projection(x, shift, scale, g, w, rope_tabs, with_kv, tile):
    b, n, _ = x.shape
    nc = w.shape[1]
    per_batch = shift.shape[0] > 1
    mod_spec = pl.BlockSpec((1, 1, D_MODEL), (lambda t, i: (i, 0, 0)) if per_batch else (lambda t, i: (0, 0, 0)))
    in_specs = [pl.BlockSpec((1, tile, D_MODEL), lambda t, i: (i, t, 0)),
                mod_spec, mod_spec,
                _const_spec((1, D_MODEL)),
                _const_spec((D_MODEL, nc))]
    args = [x, shift, scale, g.reshape(1, -1), w]
    if rope_tabs is not None:
        in_specs += [pl.BlockSpec((tile, LANES), lambda t, i: (t, 0))] * 3
        args += list(rope_tabs)

    def out(width, dtype):
        return (jax.ShapeDtypeStruct((b, n, width), dtype),
                pl.BlockSpec((1, tile, width), lambda t, i: (i, t, 0)))

    n_slabs = SSM_WIDTH // LANES
    u_out = (jax.ShapeDtypeStruct((n_slabs, n * b, LANES), F32),
             pl.BlockSpec((n_slabs, tile * b, LANES), lambda t, i: (0, t, 0)))
    outs = [out(ATTN_WIDTH, BF), out(2 * KV_WIDTH, BF), out(2 * KV_WIDTH, BF), u_out]
    if with_kv:
        outs += [out(KV_WIDTH, F32), out(KV_WIDTH, F32)]
    return pl.pallas_call(
        functools.partial(_inproj_kernel, rope=rope_tabs is not None, with_kv=with_kv, tile=tile, bsz=b),
        out_shape=[o[0] for o in outs],
        grid=(n // tile, b),
        in_specs=in_specs,
        out_specs=[o[1] for o in outs],
        compiler_params=_cparams(("arbitrary", "arbitrary")),
        name="in_projection",
    )(*args)


def _rope_tables(n_tokens):
    t = jnp.arange(n_tokens, dtype=F32)
    row = jnp.floor(t / GRID_W)
    col = t - row * GRID_W
    inv_freq = ROPE_THETA ** (-jnp.arange(ROPE_FREQS, dtype=F32) / ROPE_FREQS)
    ang_row = row[:, None] * inv_freq[None, :]
    ang_col = col[:, None] * inv_freq[None, :]
    ang = jnp.concatenate([ang_row, ang_row, ang_col, ang_col] * (LANES // HEAD_DIM), axis=-1)
    first = (jnp.arange(LANES) % (2 * ROPE_FREQS)) < ROPE_FREQS
    sin = jnp.sin(ang)
    return jnp.cos(ang), jnp.where(first, -sin, 0.0), jnp.where(first, 0.0, sin)


def _stack_heads(q_rows, j, lo):
    qa = q_rows[:, (2 * j) * LANES:(2 * j + 1) * LANES]
    qb = q_rows[:, (2 * j + 1) * LANES:(2 * j + 2) * LANES]
    keep_lo = jnp.where(lo, 1.0, 0.0).astype(BF)
    keep_hi = jnp.where(lo, 0.0, 1.0).astype(BF)
    return jnp.concatenate([qa * keep_lo, qa * keep_hi, qb * keep_lo, qb * keep_hi], axis=0)


def _sink_column(sink_ref, j, t):
    return jnp.concatenate([jnp.full((t, 1), sink_ref[4 * j + a] * LOG2E, F32) for a in range(4)], axis=0)


def _softmax_pv(scores, values, sink_col):
    m = sink_col
    for s in scores:
        m = jnp.maximum(m, jnp.max(s, axis=-1, keepdims=True))
    acc = None
    for s, v in zip(scores, values):
        o = _dot(jnp.exp2(s - m).astype(BF), v)
        acc = o if acc is None else acc + o
    den = acc[:, LANES:2 * LANES] + jnp.exp2(sink_col - m)
    return acc[:, 0:LANES] / den


def _unstack_store(o_ref, o, rows, j, lo, t):
    o_ref[0, rows, (2 * j) * LANES:(2 * j + 1) * LANES] = jnp.where(lo, o[0:t], o[t:2 * t])
    o_ref[0, rows, (2 * j + 1) * LANES:(2 * j + 2) * LANES] = jnp.where(lo, o[2 * t:3 * t], o[3 * t:4 * t])


def _fill_values(vext, r0, vv_ref):
    nr = vv_ref.shape[1]
    for j in range(KV_HEADS):
        vext[r0:r0 + nr, 2 * j * LANES:(2 * j + 1) * LANES] = vv_ref[0, :, j * LANES:(j + 1) * LANES]
        vext[r0:r0 + nr, (2 * j + 1) * LANES:(2 * j + 2) * LANES] = jnp.ones((nr, LANES), BF)


def _ctx_attn_kernel(sink_ref, q_ref, kk_ref, vv_ref, o_ref, vext, *, n):
    _fill_values(vext, 0, vv_ref)
    lo = lax.broadcasted_iota(jnp.int32, (BLOCK, LANES), 1) < HEAD_DIM
    for sub in range(n // BLOCK):
        rows = slice(sub * BLOCK, (sub + 1) * BLOCK)
        q_rows = q_ref[0, rows, :]
        for j in range(KV_HEADS):
            cols = slice(j * LANES, (j + 1) * LANES)
            qs = _stack_heads(q_rows, j, lo)
            s = _dot_nt(qs, kk_ref[0, :, cols])
            o = _softmax_pv([s], [vext[:, 2 * j * LANES:(2 * j + 2) * LANES]], _sink_column(sink_ref, j, BLOCK))
            _unstack_store(o_ref, o, rows, j, lo, BLOCK)


def _context_attention(q, kk, vv, sink):
    b, n, _ = q.shape
    return pl.pallas_call(
        functools.partial(_ctx_attn_kernel, n=n),
        out_shape=jax.ShapeDtypeStruct((b, n, ATTN_WIDTH), F32),
        grid=(b,),
        in_specs=[pl.BlockSpec(memory_space=pltpu.SMEM),
                  pl.BlockSpec((1, n, ATTN_WIDTH), lambda i: (i, 0, 0)),
                  pl.BlockSpec((1, n, 2 * KV_WIDTH), lambda i: (i, 0, 0)),
                  pl.BlockSpec((1, n, 2 * KV_WIDTH), lambda i: (i, 0, 0))],
        out_specs=pl.BlockSpec((1, n, ATTN_WIDTH), lambda i: (i, 0, 0)),
        scratch_shapes=[pltpu.VMEM((n, 4 * KV_WIDTH), BF)],
        compiler_params=_cparams(("arbitrary",)),
        name="context_attention",
    )(sink, q, kk, vv)


def _lat_attn_kernel(sink_ref, q_ref, kkm_ref, kkp_ref, kkn_ref, vvm_ref, vvp_ref, vvn_ref,
                     ckk_ref, cvv_ref, o_ref, kext, vext, cvext, *, tq, n):
    t = pl.program_id(1)
    kext[0:BLOCK] = kkp_ref[0]
    kext[BLOCK:BLOCK + tq] = kkm_ref[0]
    kext[BLOCK + tq:2 * BLOCK + tq] = kkn_ref[0]
    _fill_values(vext, 0, vvp_ref)
    _fill_values(vext, BLOCK, vvm_ref)
    _fill_values(vext, BLOCK + tq, vvn_ref)
    _fill_values(cvext, 0, cvv_ref)
    lo = lax.broadcasted_iota(jnp.int32, (BLOCK, LANES), 1) < HEAD_DIM
    r = lax.broadcasted_iota(jnp.int32, (BLOCK, 3 * BLOCK), 0)
    w = lax.broadcasted_iota(jnp.int32, (BLOCK, 3 * BLOCK), 1)
    band = (w >= r) & (w <= r + 2 * BLOCK)
    def query_block(sub, carry):
        r0 = pl.multiple_of(sub * BLOCK, BLOCK)
        rows = pl.ds(r0, BLOCK)
        key = t * tq + (sub - 1) * BLOCK + w
        valid = jnp.where(band & (key >= 0) & (key < n), 1.0, 0.0)
        valid4 = jnp.concatenate([valid] * 4, axis=0) > 0.5
        q_rows = q_ref[0, rows, :]
        win = pl.ds(r0, 3 * BLOCK)
        for j in range(KV_HEADS):
            cols = slice(j * LANES, (j + 1) * LANES)
            qs = _stack_heads(q_rows, j, lo)
            s_loc = jnp.where(valid4, _dot_nt(qs, kext[win, cols]), NEG_INF)
            s_ctx = _dot_nt(qs, ckk_ref[0, :, cols])
            vcols = slice(2 * j * LANES, (2 * j + 2) * LANES)
            o = _softmax_pv([s_loc, s_ctx], [vext[win, vcols], cvext[:, vcols]],
                            _sink_column(sink_ref, j, BLOCK))
            _unstack_store(o_ref, o, rows, j, lo, BLOCK)
        return carry

    lax.fori_loop(0, tq // BLOCK, query_block, 0)


def _latent_attention(q, kk, vv, ckk, cvv, sink, tq):
    b, n, _ = q.shape
    past = ckk.shape[1]
    nblk = n // BLOCK
    per = tq // BLOCK
    kw = 2 * KV_WIDTH
    main = pl.BlockSpec((1, tq, kw), lambda i, t: (i, t, 0))
    prev = pl.BlockSpec((1, BLOCK, kw), lambda i, t: (i, jnp.maximum(t * per - 1, 0), 0))
    nxt = pl.BlockSpec((1, BLOCK, kw), lambda i, t: (i, jnp.minimum((t + 1) * per, nblk - 1), 0))
    ctx = pl.BlockSpec((1, past, kw), lambda i, t: (i, 0, 0))
    return pl.pallas_call(
        functools.partial(_lat_attn_kernel, tq=tq, n=n),
        out_shape=jax.ShapeDtypeStruct((b, n, ATTN_WIDTH), F32),
        grid=(b, n // tq),
        in_specs=[pl.BlockSpec(memory_space=pltpu.SMEM),
                  pl.BlockSpec((1, tq, ATTN_WIDTH), lambda i, t: (i, t, 0)),
                  main, prev, nxt, main, prev, nxt, ctx, ctx],
        out_specs=pl.BlockSpec((1, tq, ATTN_WIDTH), lambda i, t: (i, t, 0)),
        scratch_shapes=[pltpu.VMEM((tq + 2 * BLOCK, kw), BF), pltpu.VMEM((tq + 2 * BLOCK, 2 * kw), BF),
                        pltpu.VMEM((past, 2 * kw), BF)],
        compiler_params=_cparams(("arbitrary", "arbitrary")),
        name="latent_attention",
    )(sink, q, kk, kk, kk, vv, vv, vv, ckk, cvv)


def _ssm_prepare(lam_re, lam_im, log_dt, b_re, b_im, c_re, c_im, d, w_glu, ntap):
    lam_re, lam_im = lam_re.astype(F32), lam_im.astype(F32)
    dt = jnp.exp(log_dt.astype(F32))[..., None]
    mag = jnp.exp(lam_re * dt)
    ang = lam_im * dt
    a_re, a_im = mag * jnp.cos(ang), mag * jnp.sin(ang)
    den = lam_re * lam_re + lam_im * lam_im
    num_re = a_re - 1.0
    coef_re = (num_re * lam_re + a_im * lam_im) / den
    coef_im = (a_im * lam_re - num_re * lam_im) / den
    br, bi = b_re.astype(F32), b_im.astype(F32)
    t_re = coef_re[..., None] * br - coef_im[..., None] * bi
    t_im = coef_re[..., None] * bi + coef_im[..., None] * br
    eye = jnp.eye(GROUPS_PER_BLOCK, dtype=F32)

    def block_diag_in(t):
        t = t.reshape(2, N_GROUP_BLOCKS, GROUPS_PER_BLOCK, SSM_STATE, SSM_CH)
        return jnp.einsum("dbgsp,gh->dbgphs", t, eye).reshape(2, N_GROUP_BLOCKS, LANES, STATE_LANES)

    taps = []
    p_re, p_im = jnp.ones_like(a_re), jnp.zeros_like(a_re)
    for _ in range(ntap):
        k_re = p_re[..., None] * t_re - p_im[..., None] * t_im
        k_im = p_re[..., None] * t_im + p_im[..., None] * t_re
        taps.append(jnp.concatenate([block_diag_in(k_re), block_diag_in(k_im)], axis=-1))
        p_re, p_im = p_re * a_re - p_im * a_im, p_re * a_im + p_im * a_re
    w_b = jnp.concatenate(taps, axis=2).astype(BF)

    def block_diag_out(c):
        c = c.astype(F32).reshape(2, N_GROUP_BLOCKS, GROUPS_PER_BLOCK, SSM_CH, SSM_STATE)
        return jnp.einsum("dbgps,gh->dbgshp", c, eye).reshape(2, N_GROUP_BLOCKS, STATE_LANES, LANES)

    w_c = jnp.concatenate([block_diag_out(c_re), -block_diag_out(c_im)], axis=2).astype(BF)

    def lanes(z):
        return z.reshape(2, N_GROUP_BLOCKS, 1, STATE_LANES)

    coef = jnp.concatenate([lanes(p_re), lanes(p_im), lanes(a_re), lanes(a_im)], axis=2)
    wg = w_glu.astype(F32).reshape(N_GROUP_BLOCKS, GROUPS_PER_BLOCK, SSM_CH, SSM_CH)
    w_g = jnp.einsum("bgpq,gh->bgphq", wg, eye).reshape(N_GROUP_BLOCKS, LANES, LANES).astype(BF)
    return w_b, w_c, coef, d.astype(F32).reshape(1, SSM_WIDTH), w_g


def _ssm_kernel(u_ref, wb_ref, wc_ref, coef_ref, h0re_ref, h0im_ref, d_ref, wg_ref,
                y_ref, fre_ref, fim_ref, ybwd, ubuf, vstate, *bufs, total, chunk, step_rows, bsz, ntap):
    bu = (bufs[0:2], bufs[2:4])
    hbf = (bufs[4:6], bufs[6:8])
    n_chunks = total // chunk
    n_steps = chunk // step_rows
    pack_steps = 2 * SUBLANES // step_rows if step_rows < 2 * SUBLANES else 1
    row = lax.broadcasted_iota(jnp.int32, (step_rows, STATE_LANES), 0)

    def project(d, c, slot):
        r0 = pl.multiple_of(c * chunk, chunk)
        u = u_ref[0, pl.ds(r0, chunk), :]
        if ntap > 1:
            if d == 0:
                ubuf[d, SUBLANES:SUBLANES + chunk] = u
            else:
                ubuf[d, 0:chunk] = u
            shifted = ubuf[d, bsz:bsz + chunk]
            if d == 0:
                ubuf[d, 0:SUBLANES] = u[chunk - SUBLANES:chunk]
            else:
                ubuf[d, chunk:chunk + SUBLANES] = u[0:SUBLANES]
            lhs = jnp.concatenate([u, shifted], axis=1)
        else:
            lhs = u
        bu[slot][d][...] = _dot(lhs.astype(BF), wb_ref[d, 0])

    def scan(slot, first):
        ar, ai, v_r, v_i, e_re, e_im = [], [], [], [], [], []
        for d in range(2):
            coef = coef_ref[d, 0]
            c0 = 0 if ntap > 1 else 2
            ar.append(jnp.broadcast_to(coef[c0:c0 + 1], (step_rows, STATE_LANES)))
            ai.append(jnp.broadcast_to(coef[c0 + 1:c0 + 2], (step_rows, STATE_LANES)))
            v_r.append(vstate[d, 0])
            v_i.append(vstate[d, 1])
            if ntap > 1:
                h0r, h0i = h0re_ref[d, 0], h0im_ref[d, 0]
                a_re, a_im = coef[2:3], coef[3:4]
                older = (row < bsz) if d == 0 else (row >= bsz)
                e_re.append(jnp.where(older, a_re * h0r - a_im * h0i, 0.0) * first)
                e_im.append(jnp.where(older, a_re * h0i + a_im * h0r, 0.0) * first)
        held = [[], []]
        for k in range(n_steps):
            for d in range(2):
                s = k if d == 0 else n_steps - 1 - k
                rows = slice(s * step_rows, (s + 1) * step_rows)
                b_r = bu[slot][d][rows, 0:STATE_LANES]
                b_i = bu[slot][d][rows, STATE_LANES:2 * STATE_LANES]
                if ntap > 1 and k == 0:
                    b_r, b_i = b_r + e_re[d], b_i + e_im[d]
                v_r[d], v_i[d] = ar[d] * v_r[d] - ai[d] * v_i[d] + b_r, ar[d] * v_i[d] + ai[d] * v_r[d] + b_i
                held[d].append((s, v_r[d], v_i[d]))
                if len(held[d]) == pack_steps:
                    hs = sorted(held[d], key=lambda h: h[0])
                    r16 = slice(hs[0][0] * step_rows, (hs[-1][0] + 1) * step_rows)
                    hbf[slot][d][r16, 0:STATE_LANES] = jnp.concatenate([h[1] for h in hs], axis=0).astype(BF)
                    hbf[slot][d][r16, STATE_LANES:2 * STATE_LANES] = jnp.concatenate(
                        [h[2] for h in hs], axis=0).astype(BF)
                    held[d] = []
        for d in range(2):
            vstate[d, 0] = v_r[d]
            vstate[d, 1] = v_i[d]

    def readout(slot, c):
        for d in range(2):
            y_part = _dot(hbf[slot][d][...], wc_ref[d, 0])
            cc = c if d == 0 else n_chunks - 1 - c
            r0 = pl.multiple_of(cc * chunk, chunk)
            if d == 0:
                y_ref[0, pl.ds(r0, chunk), :] = y_part
            else:
                ybwd[pl.ds(r0, chunk), :] = y_part

    for d in range(2):
        h0r, h0i = h0re_ref[d, 0], h0im_ref[d, 0]
        if ntap > 1:
            newer = (row >= bsz) if d == 0 else (row < bsz)
            vstate[d, 0] = jnp.where(newer, h0r, 0.0)
            vstate[d, 1] = jnp.where(newer, h0i, 0.0)
        else:
            vstate[d, 0] = h0r
            vstate[d, 1] = h0i
    if ntap > 1:
        ubuf[...] = jnp.zeros_like(ubuf)
    for d in range(2):
        hbf[1][d][...] = jnp.zeros((chunk, 2 * STATE_LANES), BF)
    project(0, 0, 0)
    project(1, n_chunks - 1, 0)

    def body(pair, carry):
        for slot in range(2):
            c = 2 * pair + slot
            first = jnp.where(c == 0, 1.0, 0.0)
            project(0, jnp.minimum(c + 1, n_chunks - 1), 1 - slot)
            project(1, jnp.maximum(n_chunks - 2 - c, 0), 1 - slot)
            readout(1 - slot, jnp.maximum(c - 1, 0))
            scan(slot, first)
        return carry

    lax.fori_loop(0, n_chunks // 2, body, 0)
    readout(1, n_chunks - 1)
    for d in range(2):
        fre_ref[d, 0] = vstate[d, 0]
        fim_ref[d, 0] = vstate[d, 1]

    fin_rows = 4 * chunk

    def finish(c, carry):
        r0 = pl.multiple_of(c * fin_rows, fin_rows)
        rows = pl.ds(r0, fin_rows)
        y = y_ref[0, rows, :] + ybwd[rows, :] + u_ref[0, rows, :] * d_ref[...]
        x = jax.nn.gelu(y, approximate=True)
        y_ref[0, rows, :] = x * jax.nn.sigmoid(_dot(x.astype(BF), wg_ref[0]))
        return carry

    lax.fori_loop(0, total // fin_rows, finish, 0)


def _s5_scan(u_tm, prep, h0_re, h0_im, bsz, ntap, rows):
    w_b, w_c, coef, d_vec, w_g = prep
    total = u_tm.shape[1]
    chunk = rows
    step_rows = bsz * ntap
    kb = LANES * ntap

    dir_spec = lambda shape: pl.BlockSpec((2, 1) + shape, lambda g: (0, g, 0, 0))
    state_shape = jax.ShapeDtypeStruct((2, N_GROUP_BLOCKS, step_rows, STATE_LANES), F32)
    return pl.pallas_call(
        functools.partial(_ssm_kernel, total=total, chunk=chunk, step_rows=step_rows, bsz=bsz, ntap=ntap),
        out_shape=[jax.ShapeDtypeStruct((N_GROUP_BLOCKS, total, LANES), F32), state_shape, state_shape],
        grid=(N_GROUP_BLOCKS,),
        in_specs=[pl.BlockSpec((1, total, LANES), lambda g: (g, 0, 0)),
                  dir_spec((kb, 2 * STATE_LANES)),
                  dir_spec((2 * STATE_LANES, LANES)),
                  dir_spec((4, STATE_LANES)),
                  dir_spec((step_rows, STATE_LANES)),
                  dir_spec((step_rows, STATE_LANES)),
                  pl.BlockSpec((1, LANES), lambda g: (0, g)),
                  pl.BlockSpec((1, LANES, LANES), lambda g: (g, 0, 0))],
        out_specs=[pl.BlockSpec((1, total, LANES), lambda g: (g, 0, 0)),
                   dir_spec((step_rows, STATE_LANES)),
                   dir_spec((step_rows, STATE_LANES))],
        scratch_shapes=[pltpu.VMEM((total, LANES), F32),
                        pltpu.VMEM((2, chunk + SUBLANES, LANES), F32),
                        pltpu.VMEM((2, 2, step_rows, STATE_LANES), F32)]
        + [pltpu.VMEM((chunk, 2 * STATE_LANES), F32)] * 4
        + [pltpu.VMEM((chunk, 2 * STATE_LANES), BF)] * 4,
        compiler_params=_cparams(("arbitrary",)),
        name="s5_scan",
    )(u_tm, w_b, w_c, coef, h0_re, h0_im, d_vec, w_g)


HALO = 16


def _merge_ffn_kernel(a_ref, ap_ref, an_ref, s_ref, sp_ref, sn_ref, x_ref, xp_ref, xn_ref,
                      gt1_ref, ga_ref, gs_ref, wo_ref,
                      sh_ref, sc_ref, gt_ref, g2_ref, wup_ref, cw_ref, cb_ref, wdn_ref, gf_ref,
                      o_ref, naext, nsext, xext, hext, act, *, tile, n_tiles, bsz):
    t = pl.program_id(0)
    bi = pl.program_id(1)
    rows = tile + 2 * HALO
    pieces = ((0, HALO, ap_ref, sp_ref, xp_ref), (HALO, tile, a_ref, s_ref, x_ref),
              (HALO + tile, HALO, an_ref, sn_ref, xn_ref))
    for r0, nr, attn_r, ssm_r, x_r in pieces:
        ssm = jnp.concatenate([ssm_r[k, pl.ds(bi, nr, stride=bsz), :] for k in range(SSM_WIDTH // LANES)], axis=1)
        naext[r0:r0 + nr] = (_rms(attn_r[0]) * ga_ref[...]).astype(BF)
        nsext[r0:r0 + nr] = (_rms(ssm) * gs_ref[...]).astype(BF)
        xext[r0:r0 + nr] = x_r[0]
    mo = _dot(naext[...], wo_ref[0:ATTN_WIDTH, :]) + _dot(nsext[...], wo_ref[ATTN_WIDTH:D_MODEL, :])
    x1 = xext[...] + gt1_ref[0] * mo
    h = _rms(x1) * g2_ref[...]
    h = (h * (1.0 + sc_ref[0]) + sh_ref[0]).astype(BF)
    x = x1[HALO:HALO + tile]
    hext[0:HALO] = jnp.where(t > 0, h[0:HALO], jnp.zeros((HALO, D_MODEL), BF))
    hext[HALO:HALO + tile] = h[HALO:HALO + tile]
    hext[HALO + tile:rows] = jnp.where(t < n_tiles - 1, h[HALO + tile:rows], jnp.zeros((HALO, D_MODEL), BF))
    h_all = hext[...]
    for c in range(D_FF // FF_CHUNK):
        halves = []
        for base in (c * FF_CHUNK, D_FF + c * FF_CHUNK):
            cols = slice(base, base + FF_CHUNK)
            up = _dot(h_all, wup_ref[:, cols])
            prev = pltpu.roll(up, 1, 0)[HALO:HALO + tile]
            nxt = pltpu.roll(up, rows - 1, 0)[HALO:HALO + tile]
            conv = (prev * cw_ref[0:1, cols] + up[HALO:HALO + tile] * cw_ref[1:2, cols]
                    + nxt * cw_ref[2:3, cols] + cb_ref[:, cols])
            halves.append(conv)
        gate, val = halves
        act[:, c * FF_CHUNK:(c + 1) * FF_CHUNK] = (gate * jax.nn.sigmoid(gate) * val).astype(BF)
    y = x + gt_ref[0] * _dot(act[...], wdn_ref[...])
    o_ref[0] = _rms(y) * gf_ref[...]


def _merge_ffn(attn, ssm, x, gate1, g_attn, g_ssm, w_out, shift, scale, gate, g2, w_up, conv_w, conv_b, w_down,
               g_final, tile):
    b, n, _ = x.shape
    n_tiles = n // tile
    per = tile // HALO
    nh = n // HALO
    n_slabs = SSM_WIDTH // LANES
    per_batch = shift.shape[0] > 1
    mod_spec = pl.BlockSpec((1, 1, D_MODEL), (lambda t, i: (i, 0, 0)) if per_batch else (lambda t, i: (0, 0, 0)))
    before = lambda t: jnp.maximum(t * per - 1, 0)
    after = lambda t: jnp.minimum((t + 1) * per, nh - 1)

    def token_specs(width):
        return [pl.BlockSpec((1, tile, width), lambda t, i: (i, t, 0)),
                pl.BlockSpec((1, HALO, width), lambda t, i: (i, before(t), 0)),
                pl.BlockSpec((1, HALO, width), lambda t, i: (i, after(t), 0))]

    slab_specs = [pl.BlockSpec((n_slabs, tile * b, LANES), lambda t, i: (0, t, 0)),
                  pl.BlockSpec((n_slabs, HALO * b, LANES), lambda t, i: (0, before(t), 0)),
                  pl.BlockSpec((n_slabs, HALO * b, LANES), lambda t, i: (0, after(t), 0))]
    rows = tile + 2 * HALO
    return pl.pallas_call(
        functools.partial(_merge_ffn_kernel, tile=tile, n_tiles=n_tiles, bsz=b),
        out_shape=jax.ShapeDtypeStruct((b, n, D_MODEL), F32),
        grid=(n_tiles, b),
        in_specs=token_specs(ATTN_WIDTH) + slab_specs + token_specs(D_MODEL)
        + [mod_spec, _const_spec((1, ATTN_WIDTH)), _const_spec((1, SSM_WIDTH)), _const_spec((D_MODEL, D_MODEL)),
           mod_spec, mod_spec, mod_spec,
           _const_spec((1, D_MODEL)),
           _const_spec((D_MODEL, 2 * D_FF)),
           _const_spec((3, 2 * D_FF)),
           _const_spec((1, 2 * D_FF)),
           _const_spec((D_FF, D_MODEL)),
           _const_spec((1, D_MODEL))],
        out_specs=pl.BlockSpec((1, tile, D_MODEL), lambda t, i: (i, t, 0)),
        scratch_shapes=[pltpu.VMEM((rows, ATTN_WIDTH), BF),
                        pltpu.VMEM((rows, SSM_WIDTH), BF),
                        pltpu.VMEM((rows, D_MODEL), F32),
                        pltpu.VMEM((rows, D_MODEL), BF),
                        pltpu.VMEM((tile, D_FF), BF)],
        compiler_params=_cparams(("arbitrary", "arbitrary")),
        name="merge_conv_ffn",
    )(attn, attn, attn, ssm, ssm, ssm, x, x, x, gate1, g_attn.reshape(1, -1), g_ssm.reshape(1, -1), w_out,
      shift, scale, gate, g2.reshape(1, -1), w_up, conv_w, conv_b.reshape(1, -1), w_down, g_final.reshape(1, -1))


def _dup_heads(w):
    a, b = w[..., :HEAD_DIM], w[..., HEAD_DIM:]
    return jnp.concatenate([a, a, b, b], axis=-1)


def _stream(x, mods, weights, rope_tabs, ctx_kv, h0, ssm_prep, with_kv, tile, ssm_rows, attn_tile):
    sh1, sc1, gt1, sh2, sc2, gt2 = mods
    (g1, g2, w_in, sink, g_attn, g_ssm, w_out, w_up, conv_w, conv_b, w_down, g_final) = weights
    b, n, _ = x.shape
    outs = _in_projection(x, sh1, sc1, g1, w_in, rope_tabs, with_kv, tile)
    q, kk, vv, u = outs[:4]
    if ctx_kv is None:
        attn = _context_attention(q, kk, vv, sink)
    else:
        attn = _latent_attention(q, kk, vv, ctx_kv[0], ctx_kv[1], sink, attn_tile)
    ntap = max(1, SUBLANES // b)
    y_tm, fin_re, fin_im = _s5_scan(u, ssm_prep, h0[0], h0[1], b, ntap, ssm_rows)
    y = _merge_ffn(attn, y_tm, x, gt1, g_attn, g_ssm, w_out, sh2, sc2, gt2, g2, w_up, conv_w, conv_b, w_down,
                   g_final, tile)
    return y, outs[4:], (fin_re, fin_im)


def kernel(x_prompt, x_sample, cache_k, cache_v, state_ssm_re, state_ssm_im, c, c_ctx, w_ada, b_ada, g_norm1, g_norm2, w_in, attn_sink, ssm_lam_re, ssm_lam_im, ssm_log_dt, ssm_b_re, ssm_b_im, ssm_c_re, ssm_c_im, ssm_d, ssm_w_glu, g_out_attn, g_out_ssm, w_out, w_up, conv_w, conv_b, w_down, g_final):
    depth = w_in.shape[0]
    assert depth == 1
    l = 0
    bp, seq, _ = x_prompt.shape
    bs, dec_seq, _ = x_sample.shape
    past = cache_k.shape[2]

    cond = jnp.concatenate([c_ctx[None, :], c, jnp.zeros((SUBLANES - 1 - bs, D_MODEL), F32)], axis=0)
    mod = _adaln(cond, w_ada[l], b_ada[l])
    mods_ctx = [mod[0:1, i * D_MODEL:(i + 1) * D_MODEL].reshape(1, 1, D_MODEL) for i in range(6)]
    mods_lat = [mod[1:1 + bs, i * D_MODEL:(i + 1) * D_MODEL].reshape(bs, 1, D_MODEL) for i in range(6)]

    wi = w_in[l]
    wq = wi[:, :ATTN_WIDTH]
    wk = wi[:, ATTN_WIDTH:ATTN_WIDTH + KV_WIDTH]
    wv = wi[:, ATTN_WIDTH + KV_WIDTH:ATTN_WIDTH + 2 * KV_WIDTH]
    wu = wi[:, ATTN_WIDTH + 2 * KV_WIDTH:]
    w_in_lat = jnp.concatenate([wq, _dup_heads(wk), _dup_heads(wv), wu], axis=1).astype(BF)
    w_in_ctx = jnp.concatenate([w_in_lat, wk.astype(BF), wv.astype(BF)], axis=1)
    weights = (g_norm1[l], g_norm2[l], None, attn_sink[l], g_out_attn[l], g_out_ssm[l], w_out[l].astype(BF),
               w_up[l].astype(BF), conv_w[l], conv_b[l], w_down[l].astype(BF), g_final)

    ssm_params = (ssm_lam_re[l], ssm_lam_im[l], ssm_log_dt[l], ssm_b_re[l], ssm_b_im[l],
                  ssm_c_re[l], ssm_c_im[l], ssm_d[l], ssm_w_glu[l])
    taps = {max(1, SUBLANES // bp), max(1, SUBLANES // bs)}
    assert taps <= {1, 2}
    ssm_prep = _ssm_prepare(*ssm_params, 2)

    zeros_h0 = jnp.zeros((2, N_GROUP_BLOCKS, bp, STATE_LANES), F32)
    w_ctx = weights[:2] + (w_in_ctx,) + weights[3:]
    y_prompt, (k_new, v_new), (fin_re, fin_im) = _stream(
        x_prompt, mods_ctx, w_ctx, None, None, (zeros_h0, zeros_h0), ssm_prep, True,
        tile=seq, ssm_rows=256, attn_tile=None)
    new_k = k_new.reshape(bp, 1, seq, KV_HEADS, HEAD_DIM)
    new_v = v_new.reshape(bp, 1, seq, KV_HEADS, HEAD_DIM)

    def states_out(fin):
        return fin.transpose(2, 0, 1, 3).reshape(bp, 1, 2, SSM_GROUPS, SSM_STATE)

    ckk = _dup_heads(cache_k[:, l].reshape(bs, past, KV_WIDTH)).astype(BF)
    cvv = _dup_heads(cache_v[:, l].reshape(bs, past, KV_WIDTH)).astype(BF)

    def h0_in(state):
        s = state.astype(F32).reshape(bs, 2, N_GROUP_BLOCKS, STATE_LANES).transpose(1, 2, 0, 3)
        return jnp.concatenate([s] * max(1, SUBLANES // bs), axis=2)

    w_lat = weights[:2] + (w_in_lat,) + weights[3:]
    y_sample, _, _ = _stream(
        x_sample, mods_lat, w_lat, _rope_tables(dec_seq), (ckk, cvv),
        (h0_in(state_ssm_re[:, l]), h0_in(state_ssm_im[:, l])), ssm_prep, False,
        tile=512, ssm_rows=256, attn_tile=512)

    return (y_prompt, y_sample, new_k, new_v, states_out(fin_re), states_out(fin_im))
```

```python
import functools
import math

import jax
import jax.numpy as jnp
from jax import lax
from jax.experimental import pallas as pl
from jax.experimental.pallas import tpu as pltpu

D_MODEL = 1024
HEAD_DIM = 64
N_HEADS = 8
KV_HEADS = 2
ATTN_WIDTH = N_HEADS * HEAD_DIM
KV_WIDTH = KV_HEADS * HEAD_DIM
BLOCK = 128
GRID_W = 64
ROPE_THETA = 10000.0
ROPE_FREQS = HEAD_DIM // 4
SSM_CH = 16
SSM_WIDTH = D_MODEL - ATTN_WIDTH
SSM_GROUPS = SSM_WIDTH // SSM_CH
SSM_STATE = 64
D_FF = 2816
EPS = 1e-6
NEG_INF = -1e30

LANES = 128
SUBLANES = 8
GROUPS_PER_BLOCK = LANES // SSM_CH
N_GROUP_BLOCKS = SSM_GROUPS // GROUPS_PER_BLOCK
STATE_LANES = GROUPS_PER_BLOCK * SSM_STATE
FF_CHUNK = 256
VMEM_LIMIT = 60 * 1024 * 1024

BF = jnp.bfloat16
F32 = jnp.float32


def _cparams(sem):
    return pltpu.CompilerParams(dimension_semantics=sem, vmem_limit_bytes=VMEM_LIMIT)


def _const_spec(shape):
    nd = len(shape)
    return pl.BlockSpec(shape, lambda *_: (0,) * nd, pipeline_mode=pl.Buffered(1))


def _rms(x):
    return x * lax.rsqrt(jnp.mean(x * x, axis=-1, keepdims=True) + EPS)


def _dot(a, b):
    return jnp.dot(a, b, preferred_element_type=F32)


def _dot_nt(a, b):
    return lax.dot_general(a, b, (((1,), (1,)), ((), ())), preferred_element_type=F32)


def _ada_kernel(cond_ref, w_ref, b_ref, o_ref):
    c = cond_ref[...]
    s = c * jax.nn.sigmoid(c)
    o_ref[...] = _dot(s.astype(BF), w_ref[...].astype(BF)) + b_ref[...]


def _adaln(cond, w_ada, b_ada):
    n = cond.shape[0]
    return pl.pallas_call(
        _ada_kernel,
        out_shape=jax.ShapeDtypeStruct((n, 6 * D_MODEL), F32),
        grid=(6,),
        in_specs=[pl.BlockSpec((n, D_MODEL), lambda i: (0, 0)),
                  pl.BlockSpec((D_MODEL, D_MODEL), lambda i: (0, i)),
                  pl.BlockSpec((1, D_MODEL), lambda i: (0, i))],
        out_specs=pl.BlockSpec((n, D_MODEL), lambda i: (0, i)),
        compiler_params=_cparams(("arbitrary",)),
        name="adaln",
    )(cond, w_ada, b_ada.reshape(1, -1))


def _rope(z, cos, sin_lo, sin_hi):
    return z * cos + pltpu.roll(z, LANES - ROPE_FREQS, 1) * sin_lo + pltpu.roll(z, ROPE_FREQS, 1) * sin_hi


def _inproj_kernel(*refs, rope, with_kv, tile, bsz):
    x_ref, sh_ref, sc_ref, g_ref, w_ref = refs[:5]
    pos = 5
    if rope:
        cos_ref, slo_ref, shi_ref = refs[pos:pos + 3]
        pos += 3
    q_ref, kk_ref, vv_ref, u_ref = refs[pos:pos + 4]
    pos += 4
    h = _rms(x_ref[0]) * g_ref[...]
    h = h * (1.0 + sc_ref[0]) + sh_ref[0]
    proj = _dot(h.astype(BF), w_ref[...])
    q = proj[:, :ATTN_WIDTH] * (HEAD_DIM ** -0.5)
    kk = proj[:, ATTN_WIDTH:ATTN_WIDTH + 2 * KV_WIDTH]
    vv = proj[:, ATTN_WIDTH + 2 * KV_WIDTH:ATTN_WIDTH + 4 * KV_WIDTH]
    u0 = ATTN_WIDTH + 4 * KV_WIDTH
    bi = pl.program_id(1)
    for k in range(SSM_WIDTH // LANES):
        u_ref[k, pl.ds(bi, tile, stride=bsz), :] = proj[:, u0 + k * LANES:u0 + (k + 1) * LANES]
    vv_ref[0] = vv.astype(BF)
    if rope:
        cos, slo, shi = cos_ref[...], slo_ref[...], shi_ref[...]
        for c in range(ATTN_WIDTH // LANES):
            q_ref[0, :, c * LANES:(c + 1) * LANES] = _rope(
                q[:, c * LANES:(c + 1) * LANES], cos, slo, shi).astype(BF)
        for c in range(2 * KV_WIDTH // LANES):
            kk_ref[0, :, c * LANES:(c + 1) * LANES] = _rope(
                kk[:, c * LANES:(c + 1) * LANES], cos, slo, shi).astype(BF)
    else:
        q_ref[0] = q.astype(BF)
        kk_ref[0] = kk.astype(BF)
    if with_kv:
        k_ref, v_ref = refs[pos:pos + 2]
        k0 = u0 + SSM_WIDTH
        k_ref[0] = proj[:, k0:k0 + KV_WIDTH]
        v_ref[0] = proj[:, k0 + KV_WIDTH:k0 + 2 * KV_WIDTH]


def _in_projection(x, shift, scale, g, w, rope_tabs, with_kv, tile):
    b, n, _ = x.shape
    nc = w.shape[1]
    per_batch = shift.shape[0] > 1
    mod_spec = pl.BlockSpec((1, 1, D_MODEL), (lambda t, i: (i, 0, 0)) if per_batch else (lambda t, i: (0, 0, 0)))
    in_specs = [pl.BlockSpec((1, tile, D_MODEL), lambda t, i: (i, t, 0)),
                mod_spec, mod_spec,
                _const_spec((1, D_MODEL)),
                _const_spec((D_MODEL, nc))]
    args = [x, shift, scale, g.reshape(1, -1), w]
    if rope_tabs is not None:
        in_specs += [pl.BlockSpec((tile, LANES), lambda t, i: (t, 0))] * 3
        args += list(rope_tabs)

    def out(width, dtype):
        return (jax.ShapeDtypeStruct((b, n, width), dtype),
                pl.BlockSpec((1, tile, width), lambda t, i: (i, t, 0)))

    n_slabs = SSM_WIDTH // LANES
    u_out = (jax.ShapeDtypeStruct((n_slabs, n * b, LANES), F32),
             pl.BlockSpec((n_slabs, tile * b, LANES), lambda t, i: (0, t, 0)))
    outs = [out(ATTN_WIDTH, BF), out(2 * KV_WIDTH, BF), out(2 * KV_WIDTH, BF), u_out]
    if with_kv:
        outs += [out(KV_WIDTH, F32), out(KV_WIDTH, F32)]
    return pl.pallas_call(
        functools.partial(_inproj_kernel, rope=rope_tabs is not None, with_kv=with_kv, tile=tile, bsz=b),
        out_shape=[o[0] for o in outs],
        grid=(n // tile, b),
        in_specs=in_specs,
        out_specs=[o[1] for o in outs],
        compiler_params=_cparams(("arbitrary", "arbitrary")),
        name="in_projection",
    )(*args)


def _rope_tables(n_tokens):
    t = jnp.arange(n_tokens, dtype=F32)
    row = jnp.floor(t / GRID_W)
    col = t - row * GRID_W
    inv_freq = ROPE_THETA ** (-jnp.arange(ROPE_FREQS, dtype=F32) / ROPE_FREQS)
    ang_row = row[:, None] * inv_freq[None, :]
    ang_col = col[:, None] * inv_freq[None, :]
    ang = jnp.concatenate([ang_row, ang_row, ang_col, ang_col] * (LANES // HEAD_DIM), axis=-1)
    first = (jnp.arange(LANES) % (2 * ROPE_FREQS)) < ROPE_FREQS
    sin = jnp.sin(ang)
    return jnp.cos(ang), jnp.where(first, -sin, 0.0), jnp.where(first, 0.0, sin)


def _stack_heads(q_rows, j, lo):
    qa = q_rows[:, (2 * j) * LANES:(2 * j + 1) * LANES]
    qb = q_rows[:, (2 * j + 1) * LANES:(2 * j + 2) * LANES]
    keep_lo = jnp.where(lo, 1.0, 0.0).astype(BF)
    keep_hi = jnp.where(lo, 0.0, 1.0).astype(BF)
    return jnp.concatenate([qa * keep_lo, qa * keep_hi, qb * keep_lo, qb * keep_hi], axis=0)


def _sink_column(sink_ref, j, t):
    return jnp.concatenate([jnp.full((t, 1), sink_ref[4 * j + a], F32) for a in range(4)], axis=0)


def _softmax_pv(scores, values, sink_col):
    m = sink_col
    for s in scores:
        m = jnp.maximum(m, jnp.max(s, axis=-1, keepdims=True))
    acc = None
    for s, v in zip(scores, values):
        o = _dot(jnp.exp(s - m).astype(BF), v)
        acc = o if acc is None else acc + o
    den = acc[:, LANES:2 * LANES] + jnp.exp(sink_col - m)
    return acc[:, 0:LANES] / den


def _unstack_store(o_ref, o, rows, j, lo, t):
    o_ref[0, rows, (2 * j) * LANES:(2 * j + 1) * LANES] = jnp.where(lo, o[0:t], o[t:2 * t])
    o_ref[0, rows, (2 * j + 1) * LANES:(2 * j + 2) * LANES] = jnp.where(lo, o[2 * t:3 * t], o[3 * t:4 * t])


def _fill_values(vext, r0, vv_ref):
    nr = vv_ref.shape[1]
    for j in range(KV_HEADS):
        vext[r0:r0 + nr, 2 * j * LANES:(2 * j + 1) * LANES] = vv_ref[0, :, j * LANES:(j + 1) * LANES]
        vext[r0:r0 + nr, (2 * j + 1) * LANES:(2 * j + 2) * LANES] = jnp.ones((nr, LANES), BF)


def _ctx_attn_kernel(sink_ref, q_ref, kk_ref, vv_ref, o_ref, vext, *, n):
    _fill_values(vext, 0, vv_ref)
    lo = lax.broadcasted_iota(jnp.int32, (BLOCK, LANES), 1) < HEAD_DIM
    for sub in range(n // BLOCK):
        rows = slice(sub * BLOCK, (sub + 1) * BLOCK)
        q_rows = q_ref[0, rows, :]
        for j in range(KV_HEADS):
            cols = slice(j * LANES, (j + 1) * LANES)
            qs = _stack_heads(q_rows, j, lo)
            s = _dot_nt(qs, kk_ref[0, :, cols])
            o = _softmax_pv([s], [vext[:, 2 * j * LANES:(2 * j + 2) * LANES]], _sink_column(sink_ref, j, BLOCK))
            _unstack_store(o_ref, o, rows, j, lo, BLOCK)


def _context_attention(q, kk, vv, sink):
    b, n, _ = q.shape
    return pl.pallas_call(
        functools.partial(_ctx_attn_kernel, n=n),
        out_shape=jax.ShapeDtypeStruct((b, n, ATTN_WIDTH), F32),
        grid=(b,),
        in_specs=[pl.BlockSpec(memory_space=pltpu.SMEM),
                  pl.BlockSpec((1, n, ATTN_WIDTH), lambda i: (i, 0, 0)),
                  pl.BlockSpec((1, n, 2 * KV_WIDTH), lambda i: (i, 0, 0)),
                  pl.BlockSpec((1, n, 2 * KV_WIDTH), lambda i: (i, 0, 0))],
        out_specs=pl.BlockSpec((1, n, ATTN_WIDTH), lambda i: (i, 0, 0)),
        scratch_shapes=[pltpu.VMEM((n, 4 * KV_WIDTH), BF)],
        compiler_params=_cparams(("arbitrary",)),
        name="context_attention",
    )(sink, q, kk, vv)


def _lat_attn_kernel(sink_ref, q_ref, kkm_ref, kkp_ref, kkn_ref, vvm_ref, vvp_ref, vvn_ref,
                     ckk_ref, cvv_ref, o_ref, kext, vext, cvext, *, tq, n):
    t = pl.program_id(1)
    kext[0:BLOCK] = kkp_ref[0]
    kext[BLOCK:BLOCK + tq] = kkm_ref[0]
    kext[BLOCK + tq:2 * BLOCK + tq] = kkn_ref[0]
    _fill_values(vext, 0, vvp_ref)
    _fill_values(vext, BLOCK, vvm_ref)
    _fill_values(vext, BLOCK + tq, vvn_ref)
    _fill_values(cvext, 0, cvv_ref)
    lo = lax.broadcasted_iota(jnp.int32, (BLOCK, LANES), 1) < HEAD_DIM
    r = lax.broadcasted_iota(jnp.int32, (BLOCK, 3 * BLOCK), 0)
    w = lax.broadcasted_iota(jnp.int32, (BLOCK, 3 * BLOCK), 1)
    band = (w >= r) & (w <= r + 2 * BLOCK)
    for sub in range(tq // BLOCK):
        rows = slice(sub * BLOCK, (sub + 1) * BLOCK)
        key = t * tq + (sub - 1) * BLOCK + w
        valid = jnp.where(band & (key >= 0) & (key < n), 1.0, 0.0)
        valid4 = jnp.concatenate([valid] * 4, axis=0) > 0.5
        q_rows = q_ref[0, rows, :]
        win = slice(sub * BLOCK, (sub + 3) * BLOCK)
        for j in range(KV_HEADS):
            cols = slice(j * LANES, (j + 1) * LANES)
            qs = _stack_heads(q_rows, j, lo)
            s_loc = jnp.where(valid4, _dot_nt(qs, kext[win, cols]), NEG_INF)
            s_ctx = _dot_nt(qs, ckk_ref[0, :, cols])
            vcols = slice(2 * j * LANES, (2 * j + 2) * LANES)
            o = _softmax_pv([s_loc, s_ctx], [vext[win, vcols], cvext[:, vcols]],
                            _sink_column(sink_ref, j, BLOCK))
            _unstack_store(o_ref, o, rows, j, lo, BLOCK)


def _latent_attention(q, kk, vv, ckk, cvv, sink, tq):
    b, n, _ = q.shape
    past = ckk.shape[1]
    nblk = n // BLOCK
    per = tq // BLOCK
    kw = 2 * KV_WIDTH
    main = pl.BlockSpec((1, tq, kw), lambda i, t: (i, t, 0))
    prev = pl.BlockSpec((1, BLOCK, kw), lambda i, t: (i, jnp.maximum(t * per - 1, 0), 0))
    nxt = pl.BlockSpec((1, BLOCK, kw), lambda i, t: (i, jnp.minimum((t + 1) * per, nblk - 1), 0))
    ctx = pl.BlockSpec((1, past, kw), lambda i, t: (i, 0, 0))
    return pl.pallas_call(
        functools.partial(_lat_attn_kernel, tq=tq, n=n),
        out_shape=jax.ShapeDtypeStruct((b, n, ATTN_WIDTH), F32),
        grid=(b, n // tq),
        in_specs=[pl.BlockSpec(memory_space=pltpu.SMEM),
                  pl.BlockSpec((1, tq, ATTN_WIDTH), lambda i, t: (i, t, 0)),
                  main, prev, nxt, main, prev, nxt, ctx, ctx],
        out_specs=pl.BlockSpec((1, tq, ATTN_WIDTH), lambda i, t: (i, t, 0)),
        scratch_shapes=[pltpu.VMEM((tq + 2 * BLOCK, kw), BF), pltpu.VMEM((tq + 2 * BLOCK, 2 * kw), BF),
                        pltpu.VMEM((past, 2 * kw), BF)],
        compiler_params=_cparams(("arbitrary", "arbitrary")),
        name="latent_attention",
    )(sink, q, kk, kk, kk, vv, vv, vv, ckk, cvv)


def _ssm_prepare(lam_re, lam_im, log_dt, b_re, b_im, c_re, c_im, d, w_glu, ntap):
    lam_re, lam_im = lam_re.astype(F32), lam_im.astype(F32)
    dt = jnp.exp(log_dt.astype(F32))[..., None]
    mag = jnp.exp(lam_re * dt)
    ang = lam_im * dt
    a_re, a_im = mag * jnp.cos(ang), mag * jnp.sin(ang)
    den = lam_re * lam_re + lam_im * lam_im
    num_re = a_re - 1.0
    coef_re = (num_re * lam_re + a_im * lam_im) / den
    coef_im = (a_im * lam_re - num_re * lam_im) / den
    br, bi = b_re.astype(F32), b_im.astype(F32)
    t_re = coef_re[..., None] * br - coef_im[..., None] * bi
    t_im = coef_re[..., None] * bi + coef_im[..., None] * br
    eye = jnp.eye(GROUPS_PER_BLOCK, dtype=F32)

    def block_diag_in(t):
        t = t.reshape(2, N_GROUP_BLOCKS, GROUPS_PER_BLOCK, SSM_STATE, SSM_CH)
        return jnp.einsum("dbgsp,gh->dbgphs", t, eye).reshape(2, N_GROUP_BLOCKS, LANES, STATE_LANES)

    taps = []
    p_re, p_im = jnp.ones_like(a_re), jnp.zeros_like(a_re)
    for _ in range(ntap):
        k_re = p_re[..., None] * t_re - p_im[..., None] * t_im
        k_im = p_re[..., None] * t_im + p_im[..., None] * t_re
        taps.append(jnp.concatenate([block_diag_in(k_re), block_diag_in(k_im)], axis=-1))
        p_re, p_im = p_re * a_re - p_im * a_im, p_re * a_im + p_im * a_re
    w_b = jnp.concatenate(taps, axis=2).astype(BF)

    def block_diag_out(c):
        c = c.astype(F32).reshape(2, N_GROUP_BLOCKS, GROUPS_PER_BLOCK, SSM_CH, SSM_STATE)
        return jnp.einsum("dbgps,gh->dbgshp", c, eye).reshape(2, N_GROUP_BLOCKS, STATE_LANES, LANES)

    w_c = jnp.concatenate([block_diag_out(c_re), -block_diag_out(c_im)], axis=2).astype(BF)

    def lanes(z):
        return z.reshape(2, N_GROUP_BLOCKS, 1, STATE_LANES)

    coef = jnp.concatenate([lanes(p_re), lanes(p_im), lanes(a_re), lanes(a_im)], axis=2)
    wg = w_glu.astype(F32).reshape(N_GROUP_BLOCKS, GROUPS_PER_BLOCK, SSM_CH, SSM_CH)
    w_g = jnp.einsum("bgpq,gh->bgphq", wg, eye).reshape(N_GROUP_BLOCKS, LANES, LANES).astype(BF)
    return w_b, w_c, coef, d.astype(F32).reshape(1, SSM_WIDTH), w_g


def _ssm_kernel(u_ref, wb_ref, wc_ref, coef_ref, h0re_ref, h0im_ref, d_ref, wg_ref,
                y_ref, fre_ref, fim_ref, ybwd, ubuf, vstate, *bufs, total, chunk, step_rows, bsz, ntap):
    bu = (bufs[0:2], bufs[2:4])
    hbf = (bufs[4:6], bufs[6:8])
    n_chunks = total // chunk
    n_steps = chunk // step_rows
    pack_steps = 2 * SUBLANES // step_rows if step_rows < 2 * SUBLANES else 1
    row = lax.broadcasted_iota(jnp.int32, (step_rows, STATE_LANES), 0)

    def project(d, c, slot):
        r0 = pl.multiple_of(c * chunk, chunk)
        u = u_ref[0, pl.ds(r0, chunk), :]
        if ntap > 1:
            if d == 0:
                ubuf[d, SUBLANES:SUBLANES + chunk] = u
            else:
                ubuf[d, 0:chunk] = u
            shifted = ubuf[d, bsz:bsz + chunk]
            if d == 0:
                ubuf[d, 0:SUBLANES] = u[chunk - SUBLANES:chunk]
            else:
                ubuf[d, chunk:chunk + SUBLANES] = u[0:SUBLANES]
            lhs = jnp.concatenate([u, shifted], axis=1)
        else:
            lhs = u
        bu[slot][d][...] = _dot(lhs.astype(BF), wb_ref[d, 0])

    def scan(slot, first):
        ar, ai, v_r, v_i, e_re, e_im = [], [], [], [], [], []
        for d in range(2):
            coef = coef_ref[d, 0]
            ar.append(jnp.broadcast_to(coef[0:1], (step_rows, STATE_LANES)))
            ai.append(jnp.broadcast_to(coef[1:2], (step_rows, STATE_LANES)))
            v_r.append(vstate[d, 0])
            v_i.append(vstate[d, 1])
            if ntap > 1:
                h0r, h0i = h0re_ref[d, 0], h0im_ref[d, 0]
                a_re, a_im = coef[2:3], coef[3:4]
                older = (row < bsz) if d == 0 else (row >= bsz)
                e_re.append(jnp.where(older, a_re * h0r - a_im * h0i, 0.0) * first)
                e_im.append(jnp.where(older, a_re * h0i + a_im * h0r, 0.0) * first)
        held = [[], []]
        for k in range(n_steps):
            for d in range(2):
                s = k if d == 0 else n_steps - 1 - k
                rows = slice(s * step_rows, (s + 1) * step_rows)
                b_r = bu[slot][d][rows, 0:STATE_LANES]
                b_i = bu[slot][d][rows, STATE_LANES:2 * STATE_LANES]
                if ntap > 1 and k == 0:
                    b_r, b_i = b_r + e_re[d], b_i + e_im[d]
                v_r[d], v_i[d] = ar[d] * v_r[d] - ai[d] * v_i[d] + b_r, ar[d] * v_i[d] + ai[d] * v_r[d] + b_i
                held[d].append((s, v_r[d], v_i[d]))
                if len(held[d]) == pack_steps:
                    hs = sorted(held[d], key=lambda h: h[0])
                    r16 = slice(hs[0][0] * step_rows, (hs[-1][0] + 1) * step_rows)
                    hbf[slot][d][r16, 0:STATE_LANES] = jnp.concatenate([h[1] for h in hs], axis=0).astype(BF)
                    hbf[slot][d][r16, STATE_LANES:2 * STATE_LANES] = jnp.concatenate(
                        [h[2] for h in hs], axis=0).astype(BF)
                    held[d] = []
        for d in range(2):
            vstate[d, 0] = v_r[d]
            vstate[d, 1] = v_i[d]

    def readout(slot, c):
        for d in range(2):
            y_part = _dot(hbf[slot][d][...], wc_ref[d, 0])
            cc = c if d == 0 else n_chunks - 1 - c
            r0 = pl.multiple_of(cc * chunk, chunk)
            if d == 0:
                y_ref[0, pl.ds(r0, chunk), :] = y_part
            else:
                ybwd[pl.ds(r0, chunk), :] = y_part

    for d in range(2):
        h0r, h0i = h0re_ref[d, 0], h0im_ref[d, 0]
        if ntap > 1:
            newer = (row >= bsz) if d == 0 else (row < bsz)
            vstate[d, 0] = jnp.where(newer, h0r, 0.0)
            vstate[d, 1] = jnp.where(newer, h0i, 0.0)
        else:
            vstate[d, 0] = h0r
            vstate[d, 1] = h0i
    if ntap > 1:
        ubuf[...] = jnp.zeros_like(ubuf)
    for d in range(2):
        hbf[1][d][...] = jnp.zeros((chunk, 2 * STATE_LANES), BF)
    project(0, 0, 0)
    project(1, n_chunks - 1, 0)

    def body(pair, carry):
        for slot in range(2):
            c = 2 * pair + slot
            first = jnp.where(c == 0, 1.0, 0.0)
            project(0, jnp.minimum(c + 1, n_chunks - 1), 1 - slot)
            project(1, jnp.maximum(n_chunks - 2 - c, 0), 1 - slot)
            readout(1 - slot, jnp.maximum(c - 1, 0))
            scan(slot, first)
        return carry

    lax.fori_loop(0, n_chunks // 2, body, 0)
    readout(1, n_chunks - 1)
    for d in range(2):
        fre_ref[d, 0] = vstate[d, 0]
        fim_ref[d, 0] = vstate[d, 1]

    fin_rows = 4 * chunk

    def finish(c, carry):
        r0 = pl.multiple_of(c * fin_rows, fin_rows)
        rows = pl.ds(r0, fin_rows)
        y = y_ref[0, rows, :] + ybwd[rows, :] + u_ref[0, rows, :] * d_ref[...]
        x = jax.nn.gelu(y, approximate=True)
        y_ref[0, rows, :] = x * jax.nn.sigmoid(_dot(x.astype(BF), wg_ref[0]))
        return carry

    lax.fori_loop(0, total // fin_rows, finish, 0)


def _s5_scan(u_tm, prep, h0_re, h0_im, bsz, ntap, rows):
    w_b, w_c, coef, d_vec, w_g = prep
    total = u_tm.shape[1]
    chunk = rows
    step_rows = bsz * ntap
    kb = LANES * ntap

    dir_spec = lambda shape: pl.BlockSpec((2, 1) + shape, lambda g: (0, g, 0, 0))
    state_shape = jax.ShapeDtypeStruct((2, N_GROUP_BLOCKS, step_rows, STATE_LANES), F32)
    return pl.pallas_call(
        functools.partial(_ssm_kernel, total=total, chunk=chunk, step_rows=step_rows, bsz=bsz, ntap=ntap),
        out_shape=[jax.ShapeDtypeStruct((N_GROUP_BLOCKS, total, LANES), F32), state_shape, state_shape],
        grid=(N_GROUP_BLOCKS,),
        in_specs=[pl.BlockSpec((1, total, LANES), lambda g: (g, 0, 0)),
                  dir_spec((kb, 2 * STATE_LANES)),
                  dir_spec((2 * STATE_LANES, LANES)),
                  dir_spec((4, STATE_LANES)),
                  dir_spec((step_rows, STATE_LANES)),
                  dir_spec((step_rows, STATE_LANES)),
                  pl.BlockSpec((1, LANES), lambda g: (0, g)),
                  pl.BlockSpec((1, LANES, LANES), lambda g: (g, 0, 0))],
        out_specs=[pl.BlockSpec((1, total, LANES), lambda g: (g, 0, 0)),
                   dir_spec((step_rows, STATE_LANES)),
                   dir_spec((step_rows, STATE_LANES))],
        scratch_shapes=[pltpu.VMEM((total, LANES), F32),
                        pltpu.VMEM((2, chunk + SUBLANES, LANES), F32),
                        pltpu.VMEM((2, 2, step_rows, STATE_LANES), F32)]
        + [pltpu.VMEM((chunk, 2 * STATE_LANES), F32)] * 4
        + [pltpu.VMEM((chunk, 2 * STATE_LANES), BF)] * 4,
        compiler_params=_cparams(("arbitrary",)),
        name="s5_scan",
    )(u_tm, w_b, w_c, coef, h0_re, h0_im, d_vec, w_g)


HALO = 16


def _merge_ffn_kernel(a_ref, ap_ref, an_ref, s_ref, sp_ref, sn_ref, x_ref, xp_ref, xn_ref,
                      gt1_ref, ga_ref, gs_ref, wo_ref,
                      sh_ref, sc_ref, gt_ref, g2_ref, wup_ref, cw_ref, cb_ref, wdn_ref, gf_ref,
                      o_ref, naext, nsext, xext, hext, act, *, tile, n_tiles, bsz):
    t = pl.program_id(0)
    bi = pl.program_id(1)
    rows = tile + 2 * HALO
    pieces = ((0, HALO, ap_ref, sp_ref, xp_ref), (HALO, tile, a_ref, s_ref, x_ref),
              (HALO + tile, HALO, an_ref, sn_ref, xn_ref))
    for r0, nr, attn_r, ssm_r, x_r in pieces:
        ssm = jnp.concatenate([ssm_r[k, pl.ds(bi, nr, stride=bsz), :] for k in range(SSM_WIDTH // LANES)], axis=1)
        naext[r0:r0 + nr] = (_rms(attn_r[0]) * ga_ref[...]).astype(BF)
        nsext[r0:r0 + nr] = (_rms(ssm) * gs_ref[...]).astype(BF)
        xext[r0:r0 + nr] = x_r[0]
    mo = _dot(naext[...], wo_ref[0:ATTN_WIDTH, :]) + _dot(nsext[...], wo_ref[ATTN_WIDTH:D_MODEL, :])
    x1 = xext[...] + gt1_ref[0] * mo
    h = _rms(x1) * g2_ref[...]
    h = (h * (1.0 + sc_ref[0]) + sh_ref[0]).astype(BF)
    x = x1[HALO:HALO + tile]
    hext[0:HALO] = jnp.where(t > 0, h[0:HALO], jnp.zeros((HALO, D_MODEL), BF))
    hext[HALO:HALO + tile] = h[HALO:HALO + tile]
    hext[HALO + tile:rows] = jnp.where(t < n_tiles - 1, h[HALO + tile:rows], jnp.zeros((HALO, D_MODEL), BF))
    h_all = hext[...]
    for c in range(D_FF // FF_CHUNK):
        halves = []
        for base in (c * FF_CHUNK, D_FF + c * FF_CHUNK):
            cols = slice(base, base + FF_CHUNK)
            up = _dot(h_all, wup_ref[:, cols])
            prev = pltpu.roll(up, 1, 0)[HALO:HALO + tile]
            nxt = pltpu.roll(up, rows - 1, 0)[HALO:HALO + tile]
            conv = (prev * cw_ref[0:1, cols] + up[HALO:HALO + tile] * cw_ref[1:2, cols]
                    + nxt * cw_ref[2:3, cols] + cb_ref[:, cols])
            halves.append(conv)
        gate, val = halves
        act[:, c * FF_CHUNK:(c + 1) * FF_CHUNK] = (gate * jax.nn.sigmoid(gate) * val).astype(BF)
    y = x + gt_ref[0] * _dot(act[...], wdn_ref[...])
    o_ref[0] = _rms(y) * gf_ref[...]


def _merge_ffn(attn, ssm, x, gate1, g_attn, g_ssm, w_out, shift, scale, gate, g2, w_up, conv_w, conv_b, w_down,
               g_final, tile):
    b, n, _ = x.shape
    n_tiles = n // tile
    per = tile // HALO
    nh = n // HALO
    n_slabs = SSM_WIDTH // LANES
    per_batch = shift.shape[0] > 1
    mod_spec = pl.BlockSpec((1, 1, D_MODEL), (lambda t, i: (i, 0, 0)) if per_batch else (lambda t, i: (0, 0, 0)))
    before = lambda t: jnp.maximum(t * per - 1, 0)
    after = lambda t: jnp.minimum((t + 1) * per, nh - 1)

    def token_specs(width):
        return [pl.BlockSpec((1, tile, width), lambda t, i: (i, t, 0)),
                pl.BlockSpec((1, HALO, width), lambda t, i: (i, before(t), 0)),
                pl.BlockSpec((1, HALO, width), lambda t, i: (i, after(t), 0))]

    slab_specs = [pl.BlockSpec((n_slabs, tile * b, LANES), lambda t, i: (0, t, 0)),
                  pl.BlockSpec((n_slabs, HALO * b, LANES), lambda t, i: (0, before(t), 0)),
                  pl.BlockSpec((n_slabs, HALO * b, LANES), lambda t, i: (0, after(t), 0))]
    rows = tile + 2 * HALO
    return pl.pallas_call(
        functools.partial(_merge_ffn_kernel, tile=tile, n_tiles=n_tiles, bsz=b),
        out_shape=jax.ShapeDtypeStruct((b, n, D_MODEL), F32),
        grid=(n_tiles, b),
        in_specs=token_specs(ATTN_WIDTH) + slab_specs + token_specs(D_MODEL)
        + [mod_spec, _const_spec((1, ATTN_WIDTH)), _const_spec((1, SSM_WIDTH)), _const_spec((D_MODEL, D_MODEL)),
           mod_spec, mod_spec, mod_spec,
           _const_spec((1, D_MODEL)),
           _const_spec((D_MODEL, 2 * D_FF)),
           _const_spec((3, 2 * D_FF)),
           _const_spec((1, 2 * D_FF)),
           _const_spec((D_FF, D_MODEL)),
           _const_spec((1, D_MODEL))],
        out_specs=pl.BlockSpec((1, tile, D_MODEL), lambda t, i: (i, t, 0)),
        scratch_shapes=[pltpu.VMEM((rows, ATTN_WIDTH), BF),
                        pltpu.VMEM((rows, SSM_WIDTH), BF),
                        pltpu.VMEM((rows, D_MODEL), F32),
                        pltpu.VMEM((rows, D_MODEL), BF),
                        pltpu.VMEM((tile, D_FF), BF)],
        compiler_params=_cparams(("arbitrary", "arbitrary")),
        name="merge_conv_ffn",
    )(attn, attn, attn, ssm, ssm, ssm, x, x, x, gate1, g_attn.reshape(1, -1), g_ssm.reshape(1, -1), w_out,
      shift, scale, gate, g2.reshape(1, -1), w_up, conv_w, conv_b.reshape(1, -1), w_down, g_final.reshape(1, -1))


def _dup_heads(w):
    a, b = w[..., :HEAD_DIM], w[..., HEAD_DIM:]
    return jnp.concatenate([a, a, b, b], axis=-1)


def _stream(x, mods, weights, rope_tabs, ctx_kv, h0, ssm_prep, with_kv, tile, ssm_rows, attn_tile):
    sh1, sc1, gt1, sh2, sc2, gt2 = mods
    (g1, g2, w_in, sink, g_attn, g_ssm, w_out, w_up, conv_w, conv_b, w_down, g_final) = weights
    b, n, _ = x.shape
    outs = _in_projection(x, sh1, sc1, g1, w_in, rope_tabs, with_kv, tile)
    q, kk, vv, u = outs[:4]
    if ctx_kv is None:
        attn = _context_attention(q, kk, vv, sink)
    else:
        attn = _latent_attention(q, kk, vv, ctx_kv[0], ctx_kv[1], sink, attn_tile)
    ntap = max(1, SUBLANES // b)
    y_tm, fin_re, fin_im = _s5_scan(u, ssm_prep[ntap], h0[0], h0[1], b, ntap, ssm_rows)
    y = _merge_ffn(attn, y_tm, x, gt1, g_attn, g_ssm, w_out, sh2, sc2, gt2, g2, w_up, conv_w, conv_b, w_down,
                   g_final, tile)
    return y, outs[4:], (fin_re, fin_im)


def kernel(x_prompt, x_sample, cache_k, cache_v, state_ssm_re, state_ssm_im, c, c_ctx, w_ada, b_ada, g_norm1, g_norm2, w_in, attn_sink, ssm_lam_re, ssm_lam_im, ssm_log_dt, ssm_b_re, ssm_b_im, ssm_c_re, ssm_c_im, ssm_d, ssm_w_glu, g_out_attn, g_out_ssm, w_out, w_up, conv_w, conv_b, w_down, g_final):
    depth = w_in.shape[0]
    assert depth == 1
    l = 0
    bp, seq, _ = x_prompt.shape
    bs, dec_seq, _ = x_sample.shape
    past = cache_k.shape[2]

    cond = jnp.concatenate([c_ctx[None, :], c, jnp.zeros((SUBLANES - 1 - bs, D_MODEL), F32)], axis=0)
    mod = _adaln(cond, w_ada[l], b_ada[l])
    mods_ctx = [mod[0:1, i * D_MODEL:(i + 1) * D_MODEL].reshape(1, 1, D_MODEL) for i in range(6)]
    mods_lat = [mod[1:1 + bs, i * D_MODEL:(i + 1) * D_MODEL].reshape(bs, 1, D_MODEL) for i in range(6)]

    wi = w_in[l]
    wq = wi[:, :ATTN_WIDTH]
    wk = wi[:, ATTN_WIDTH:ATTN_WIDTH + KV_WIDTH]
    wv = wi[:, ATTN_WIDTH + KV_WIDTH:ATTN_WIDTH + 2 * KV_WIDTH]
    wu = wi[:, ATTN_WIDTH + 2 * KV_WIDTH:]
    w_in_lat = jnp.concatenate([wq, _dup_heads(wk), _dup_heads(wv), wu], axis=1).astype(BF)
    w_in_ctx = jnp.concatenate([w_in_lat, wk.astype(BF), wv.astype(BF)], axis=1)
    weights = (g_norm1[l], g_norm2[l], None, attn_sink[l], g_out_attn[l], g_out_ssm[l], w_out[l].astype(BF),
               w_up[l].astype(BF), conv_w[l], conv_b[l], w_down[l].astype(BF), g_final)

    ssm_params = (ssm_lam_re[l], ssm_lam_im[l], ssm_log_dt[l], ssm_b_re[l], ssm_b_im[l],
                  ssm_c_re[l], ssm_c_im[l], ssm_d[l], ssm_w_glu[l])
    taps = {max(1, SUBLANES // bp), max(1, SUBLANES // bs)}
    ssm_prep = {nt: _ssm_prepare(*ssm_params, nt) for nt in taps}

    zeros_h0 = jnp.zeros((2, N_GROUP_BLOCKS, bp, STATE_LANES), F32)
    w_ctx = weights[:2] + (w_in_ctx,) + weights[3:]
    y_prompt, (k_new, v_new), (fin_re, fin_im) = _stream(
        x_prompt, mods_ctx, w_ctx, None, None, (zeros_h0, zeros_h0), ssm_prep, True,
        tile=seq, ssm_rows=256, attn_tile=None)
    new_k = k_new.reshape(bp, 1, seq, KV_HEADS, HEAD_DIM)
    new_v = v_new.reshape(bp, 1, seq, KV_HEADS, HEAD_DIM)

    def states_out(fin):
        return fin.transpose(2, 0, 1, 3).reshape(bp, 1, 2, SSM_GROUPS, SSM_STATE)

    ckk = _dup_heads(cache_k[:, l].reshape(bs, past, KV_WIDTH)).astype(BF)
    cvv = _dup_heads(cache_v[:, l].reshape(bs, past, KV_WIDTH)).astype(BF)

    def h0_in(state):
        s = state.astype(F32).reshape(bs, 2, N_GROUP_BLOCKS, STATE_LANES).transpose(1, 2, 0, 3)
        return jnp.concatenate([s] * max(1, SUBLANES // bs), axis=2)

    w_lat = weights[:2] + (w_in_lat,) + weights[3:]
    y_sample, _, _ = _stream(
        x_sample, mods_lat, w_lat, _rope_tables(dec_seq), (ckk, cvv),
        (h0_in(state_ssm_re[:, l]), h0_in(state_ssm_im[:, l])), ssm_prep, False,
        tile=512, ssm_rows=512, attn_tile=512)

    return (y_prompt, y_sample, new_k, new_v, states_out(fin_re), states_out(fin_im))
```
